```python
import math
import jax
import jax.numpy as jnp
from jax import lax
import numpy as np


D_MODEL = 1024
BATCH = 16
SEQ = 4096
DEPTH = 4

GRID_W = 64
CTX_LEN = 256
HY_W = 256
CF_W = 256
DA_HEADS = 4
DA_HD = 64
DA_VD = 2 * DA_HD
DA_W = DA_HEADS * DA_VD
MIX_W = HY_W + CF_W + DA_W
HY_IN = 3 * HY_W
CF_IN = 2 * CF_W
DA_QK = DA_HEADS * 2 * DA_HD
OFF_CF = HY_IN
OFF_Q = OFF_CF + CF_IN
OFF_K = OFF_Q + DA_QK
OFF_V = OFF_K + DA_QK
IN_W = OFF_V + DA_W
SHORT_CONV = 3
CF_CONV = 31
HY_BANDS = 16
HY_EMB = 1 + 2 * HY_BANDS
HY_FF = 64
HY_MIN_DECAY = math.log(1e-2) / 1.5
HY_MAX_DECAY = math.log(1e-2) / 0.3
ROPE_THETA = 10000.0
ROPE_AXIS_FREQS = DA_HD // 4
Q_BLOCK = 128
N_EXPERTS = 16
N_GROUPS = 4
EXPERTS_PER_GROUP = N_EXPERTS // N_GROUPS
TOPK_GROUPS = 1
GROUP_SCORE_K = 2
TOP_K = 2
D_EXPERT = 512
EPS = 1e-6

kernel_name = 'hybrid_diffusion_hyena_conformer_diffattn_moe'


def _rms_norm(x, g):
    xf = x.astype(jnp.float32)
    y = xf * lax.rsqrt(jnp.mean(xf * xf, axis=-1, keepdims=True) + EPS)
    return (y * g.astype(jnp.float32)).astype(x.dtype)


def _layer_norm(x, g, b):
    xf = x.astype(jnp.float32)
    mu = jnp.mean(xf, axis=-1, keepdims=True)
    xc = xf - mu
    y = xc * lax.rsqrt(jnp.mean(xc * xc, axis=-1, keepdims=True) + EPS)
    return (y * g.astype(jnp.float32) + b.astype(jnp.float32)).astype(x.dtype)


def _modulate(h, shift, scale):
    return h * (1.0 + scale) + shift


def _dw_conv(u, w, b):
    k = w.shape[0]
    y = lax.conv_general_dilated(u, w.astype(u.dtype)[:, None, :], window_strides=(1,),
                                 padding=[(k // 2, k // 2)],
                                 dimension_numbers=('NWC', 'WIO', 'NWC'),
                                 feature_group_count=u.shape[-1])
    return y + b.astype(u.dtype)


def _hyena_filter(L, w1, b1, w2, b2, w3, freq):
    f32 = jnp.float32
    t_idx = jnp.arange(L, dtype=f32)[:, None]
    t01 = t_idx / max(L - 1, 1)
    bands = jnp.linspace(1e-4, HY_BANDS - 1, HY_BANDS, dtype=f32)
    ang = 2.0 * math.pi * bands * t_idx / L
    feats = jnp.concatenate([t01, jnp.cos(ang), -jnp.sin(ang)], axis=-1)
    freq = freq.astype(f32)
    hid = jnp.sin(freq * (feats @ w1.astype(f32) + b1.astype(f32)))
    hid = jnp.sin(freq * (hid @ w2.astype(f32) + b2.astype(f32)))
    h = (hid @ w3.astype(f32)).reshape(L, 2, HY_W)
    deltas = jnp.abs(jnp.linspace(HY_MIN_DECAY, HY_MAX_DECAY, HY_W, dtype=f32))
    h = h * jnp.exp(-t01 * deltas)[:, None, :]
    return jnp.concatenate([h[:, 0], jnp.zeros((1, HY_W), f32), h[:0:-1, 1]], axis=0)


def _hyena(u, conv_w, conv_b, w1, b1, w2, b2, w3, freq, bias):
    L = u.shape[1]
    u = _dw_conv(u, conv_w, conv_b)
    x0, x1, v = jnp.split(u, 3, axis=-1)
    z = (x1 * v).astype(jnp.float32)
    kern = _hyena_filter(L, w1, b1, w2, b2, w3, freq)
    zf = jnp.fft.rfft(z, n=2 * L, axis=1)
    kf = jnp.fft.rfft(kern, axis=0)
    y = jnp.fft.irfft(zf * kf[None], n=2 * L, axis=1)[:, :L]
    y = y + z * bias.astype(jnp.float32)
    return x0 * y.astype(u.dtype)


def _conformer_conv(u, dw_w, dw_b, ln_g, ln_b):
    a, g = jnp.split(u, 2, axis=-1)
    y = _dw_conv(a * jax.nn.sigmoid(g), dw_w, dw_b)
    return jax.nn.silu(_layer_norm(y, ln_g, ln_b))


def _rope_tables(n_lat):
    rows = n_lat // GRID_W
    row = jnp.repeat(jnp.arange(rows, dtype=jnp.float32), GRID_W)
    col = jnp.tile(jnp.arange(GRID_W, dtype=jnp.float32), rows)
    inv = ROPE_THETA ** (-jnp.arange(ROPE_AXIS_FREQS, dtype=jnp.float32) / ROPE_AXIS_FREQS)
    ang = jnp.concatenate([row[:, None] * inv, col[:, None] * inv], axis=-1)
    return jnp.cos(ang), jnp.sin(ang)


def _rope(t, cos, sin):
    t1, t2 = jnp.split(t, 2, axis=-1)
    c = cos[:, None, None, :].astype(t.dtype)
    s = sin[:, None, None, :].astype(t.dtype)
    return jnp.concatenate([t1 * c - t2 * s, t2 * c + t1 * s], axis=-1)


def _qk_heads(t, gain, cos=None, sin=None):
    B, L = t.shape[:2]
    t = _rms_norm(t.reshape(B, L, DA_HEADS, 2, DA_HD), gain)
    if cos is not None:
        t = _rope(t, cos, sin)
    return t


def _v_heads(t):
    B, L = t.shape[:2]
    return t.reshape(B, L, DA_HEADS, DA_VD)


def _diff_attn(q, k, v, lam, lam_init, subln_g):
    B, Lq = q.shape[:2]
    nb = Lq // Q_BLOCK
    qb = jnp.moveaxis(q.reshape(B, nb, Q_BLOCK, DA_HEADS, 2, DA_HD), 1, 0)
    scale = DA_HD ** -0.5

    def block(qi):
        s = jnp.einsum('bqhmd,bkhmd->bhmqk', qi, k).astype(jnp.float32) * scale
        p = jax.nn.softmax(s, axis=-1)
        a = p[:, :, 0] - lam * p[:, :, 1]
        return jnp.einsum('bhqk,bkhv->bqhv', a.astype(v.dtype), v)

    o = lax.map(block, qb)
    o = jnp.moveaxis(o, 0, 1).reshape(B, Lq, DA_HEADS, DA_VD)
    o = _rms_norm(o, subln_g) * (1.0 - lam_init)
    return o.reshape(B, Lq, DA_W)


def _moe(h, w_router, b_router, w1, w3, w2):
    T = h.shape[0]
    f32 = jnp.float32
    scores = jax.nn.sigmoid((h @ w_router).astype(f32))
    sel = scores + b_router.astype(f32)
    g_score = lax.top_k(sel.reshape(T, N_GROUPS, EXPERTS_PER_GROUP), GROUP_SCORE_K)[0].sum(-1)
    _, g_idx = lax.top_k(g_score, TOPK_GROUPS)
    g_mask = jax.nn.one_hot(g_idx, N_GROUPS, dtype=f32).sum(1)
    e_mask = jnp.repeat(g_mask, EXPERTS_PER_GROUP, axis=-1) > 0
    _, e_idx = lax.top_k(jnp.where(e_mask, sel, -jnp.inf), TOP_K)
    w = jnp.take_along_axis(scores, e_idx, axis=-1)
    w = w / jnp.sum(w, axis=-1, keepdims=True)
    gates = jnp.sum(jax.nn.one_hot(e_idx, N_EXPERTS, dtype=f32) * w[..., None], axis=1).astype(h.dtype)
    y = jnp.zeros_like(h)
    for e in range(N_EXPERTS):
        a = jax.nn.silu(h @ w1[e]) * (h @ w3[e])
        y = y + gates[:, e:e + 1] * (a @ w2[e])
    return y


def setup_inputs(seed: int = 0) -> dict:
    key = jax.random.key(seed)
    ks = iter(jax.random.split(key, 40))

    def nrm(shape, scale=1.0):
        return scale * jax.random.normal(next(ks), shape, dtype=jnp.float32)

    def gain(shape):
        return 1.0 + nrm(shape, 0.02)

    D = D_MODEL
    return {
        'x': nrm((BATCH, SEQ, D)),
        'c': nrm((BATCH, D)),
        'ctx': nrm((BATCH, CTX_LEN, D)),
        'c_ctx': nrm((D,)),
        'w_ada': nrm((DEPTH, D, 6 * D), 0.5 * D ** -0.5),
        'b_ada': nrm((DEPTH, 6 * D), 0.02),
        'norm1_g': gain((DEPTH, D)),
        'norm2_g': gain((DEPTH, D)),
        'w_in': nrm((DEPTH, D, IN_W), D ** -0.5),
        'hy_conv_w': nrm((DEPTH, SHORT_CONV, HY_IN), SHORT_CONV ** -0.5),
        'hy_conv_b': nrm((DEPTH, HY_IN), 0.02),
        'hy_w1': nrm((DEPTH, HY_EMB, HY_FF), HY_EMB ** -0.5),
        'hy_b1': nrm((DEPTH, HY_FF), 0.1),
        'hy_w2': nrm((DEPTH, HY_FF, HY_FF), HY_FF ** -0.5),
        'hy_b2': nrm((DEPTH, HY_FF), 0.1),
        'hy_w3': nrm((DEPTH, HY_FF, 2 * HY_W), 0.005),
        'hy_freq': gain((DEPTH, HY_FF)),
        'hy_bias': nrm((DEPTH, HY_W), 0.5),
        'cf_dw_w': nrm((DEPTH, CF_CONV, CF_W), CF_CONV ** -0.5),
        'cf_dw_b': nrm((DEPTH, CF_W), 0.02),
        'cf_ln_g': gain((DEPTH, CF_W)),
        'cf_ln_b': nrm((DEPTH, CF_W), 0.02),
        'da_qn_g': gain((DEPTH, DA_HD)),
        'da_kn_g': gain((DEPTH, DA_HD)),
        'da_lam': nrm((DEPTH, 4, DA_HD), 0.1),
        'da_subln_g': gain((DEPTH, DA_VD)),
        'w_out': nrm((DEPTH, MIX_W, D), MIX_W ** -0.5),
        'w_router': nrm((D, N_EXPERTS), D ** -0.5),
        'b_router': nrm((N_EXPERTS,), 0.01),
        'moe_w1': nrm((DEPTH, N_EXPERTS, D, D_EXPERT), D ** -0.5),
        'moe_w3': nrm((DEPTH, N_EXPERTS, D, D_EXPERT), D ** -0.5),
        'moe_w2': nrm((DEPTH, N_EXPERTS, D_EXPERT, D), D_EXPERT ** -0.5),
    }


def reference(x, c, ctx, c_ctx, w_ada, b_ada, norm1_g, norm2_g, w_in, hy_conv_w, hy_conv_b,
              hy_w1, hy_b1, hy_w2, hy_b2, hy_w3, hy_freq, hy_bias, cf_dw_w, cf_dw_b, cf_ln_g,
              cf_ln_b, da_qn_g, da_kn_g, da_lam, da_subln_g, w_out, w_router, b_router,
              moe_w1, moe_w3, moe_w2):
    f32 = jnp.float32
    B, n_lat, D = x.shape
    n_ctx = ctx.shape[1]
    cos, sin = _rope_tables(n_lat)
    sc = jax.nn.silu(c)
    scc = jax.nn.silu(c_ctx)[None]
    x_lat, x_ctx = x, ctx
    for l in range(DEPTH):
        last = l == DEPTH - 1
        lam_init = 0.8 - 0.6 * math.exp(-0.3 * l)
        mod = (sc @ w_ada[l] + b_ada[l]).reshape(B, 6, 1, D)
        mod_c = (scc @ w_ada[l] + b_ada[l]).reshape(6, 1, D)
        lp = da_lam[l].astype(f32)
        lam = jnp.exp(jnp.sum(lp[0] * lp[1])) - jnp.exp(jnp.sum(lp[2] * lp[3])) + lam_init
        hy_p = (hy_conv_w[l], hy_conv_b[l], hy_w1[l], hy_b1[l], hy_w2[l], hy_b2[l], hy_w3[l],
                hy_freq[l], hy_bias[l])
        cf_p = (cf_dw_w[l], cf_dw_b[l], cf_ln_g[l], cf_ln_b[l])

        h = _modulate(_rms_norm(x_lat, norm1_g[l]), mod[:, 0], mod[:, 1])
        hc = _modulate(_rms_norm(x_ctx, norm1_g[l]), mod_c[0], mod_c[1])
        u = h @ w_in[l]
        hy_lat = _hyena(u[..., :OFF_CF], *hy_p)
        cf_lat = _conformer_conv(u[..., OFF_CF:OFF_Q], *cf_p)
        q_lat = _qk_heads(u[..., OFF_Q:OFF_K], da_qn_g[l], cos, sin)
        k_lat = _qk_heads(u[..., OFF_K:OFF_V], da_kn_g[l], cos, sin)
        v_lat = _v_heads(u[..., OFF_V:])
        uc_kv = hc @ w_in[l][:, OFF_K:]
        k_ctx = _qk_heads(uc_kv[..., :DA_QK], da_kn_g[l])
        v_ctx = _v_heads(uc_kv[..., DA_QK:])
        k_all = jnp.concatenate([k_ctx, k_lat], axis=1)
        v_all = jnp.concatenate([v_ctx, v_lat], axis=1)
        da_lat = _diff_attn(q_lat, k_all, v_all, lam, lam_init, da_subln_g[l])
        y = jnp.concatenate([hy_lat, cf_lat, da_lat], axis=-1) @ w_out[l]
        if not last:
            uc = hc @ w_in[l][:, :OFF_K]
            hy_ctx = _hyena(uc[..., :OFF_CF], *hy_p)
            cf_ctx = _conformer_conv(uc[..., OFF_CF:OFF_Q], *cf_p)
            q_ctx = _qk_heads(uc[..., OFF_Q:], da_qn_g[l])
            da_ctx = _diff_attn(q_ctx, k_ctx, v_ctx, lam, lam_init, da_subln_g[l])
            yc = jnp.concatenate([hy_ctx, cf_ctx, da_ctx], axis=-1) @ w_out[l]
            x_ctx = x_ctx + mod_c[2] * yc
        x_lat = x_lat + mod[:, 2] * y

        h2 = _modulate(_rms_norm(x_lat, norm2_g[l]), mod[:, 3], mod[:, 4]).reshape(B * n_lat, D)
        if last:
            f = _moe(h2, w_router, b_router, moe_w1[l], moe_w3[l], moe_w2[l])
            x_lat = x_lat + mod[:, 5] * f.reshape(B, n_lat, D)
        else:
            h2c = _modulate(_rms_norm(x_ctx, norm2_g[l]), mod_c[3], mod_c[4]).reshape(B * n_ctx, D)
            f = _moe(jnp.concatenate([h2, h2c], axis=0), w_router, b_router,
                     moe_w1[l], moe_w3[l], moe_w2[l])
            x_lat = x_lat + mod[:, 5] * f[:B * n_lat].reshape(B, n_lat, D)
            x_ctx = x_ctx + mod_c[5] * f[B * n_lat:].reshape(B, n_ctx, D)
    return x_lat
```

```python
import functools
import math

import jax
import jax.numpy as jnp
from jax import lax
from jax.experimental import pallas as pl
from jax.experimental.pallas import tpu as pltpu

f32 = jnp.float32
bf16 = jnp.bfloat16
i32 = jnp.int32

D_MODEL = 1024
GRID_W = 64
HY_W = 256
CF_W = 256
DA_HEADS = 4
DA_HD = 64
DA_VD = 2 * DA_HD
DA_W = DA_HEADS * DA_VD
MIX_W = HY_W + CF_W + DA_W
HY_IN = 3 * HY_W
CF_IN = 2 * CF_W
DA_QK = DA_HEADS * 2 * DA_HD
OFF_CF = HY_IN
OFF_Q = OFF_CF + CF_IN
OFF_K = OFF_Q + DA_QK
OFF_V = OFF_K + DA_QK
IN_W = OFF_V + DA_W
CF_CONV = 31
HY_BANDS = 16
HY_EMB = 1 + 2 * HY_BANDS
HY_EMB_PAD = 64
HY_FF = 64
HY_MIN_DECAY = math.log(1e-2) / 1.5
HY_MAX_DECAY = math.log(1e-2) / 0.3
ROPE_THETA = 10000.0
ROPE_AXIS_FREQS = DA_HD // 4
N_EXPERTS = 16
N_GROUPS = 4
EXPERTS_PER_GROUP = N_EXPERTS // N_GROUPS
D_EXPERT = 512
EPS = 1e-6

VMEM_LIMIT_BYTES = 56 * 1024 * 1024
CF_HALO = 16
BF16_ROWS = 16


def _params(*sem):
    return pltpu.CompilerParams(dimension_semantics=sem, vmem_limit_bytes=VMEM_LIMIT_BYTES)


def _dot(a, b):
    return jnp.dot(a, b, preferred_element_type=f32)


def _dot_nt(a, b):
    return lax.dot_general(a, b, (((1,), (1,)), ((), ())), preferred_element_type=f32)


def _split_bf16(v):
    hi = v.astype(bf16)
    lo = (v - hi.astype(f32)).astype(bf16)
    return hi, lo


def _ada_body(c_ref, w_ref, b_ref, o_ref):
    c = c_ref[...]
    s = c * jax.nn.sigmoid(c)
    o_ref[0] = _dot(s, w_ref[0]) + b_ref[0]


def _ada(cc, w_ada, b_ada):
    depth, d, n = w_ada.shape
    r = cc.shape[0]
    tn = 1536
    return pl.pallas_call(
        _ada_body,
        grid=(depth, n // tn),
        in_specs=[pl.BlockSpec((r, d), lambda l, j: (0, 0)),
                  pl.BlockSpec((1, d, tn), lambda l, j: (l, 0, j)),
                  pl.BlockSpec((1, 1, tn), lambda l, j: (l, 0, j))],
        out_specs=pl.BlockSpec((1, r, tn), lambda l, j: (l, 0, j)),
        out_shape=jax.ShapeDtypeStruct((depth, r, n), f32),
        compiler_params=_params("parallel", "parallel"),
        name="ada_mod",
    )(cc, w_ada, b_ada.reshape(depth, 1, n))


def _inproj_body(rope, x_ref, sh_ref, sc_ref, g_ref, w_ref, qg_ref, kg_ref, bd_ref, *rest):
    if rope:
        cos_ref, sin_ref, hy_ref, cf_ref, q_ref, k_ref, v_ref = rest
    else:
        hy_ref, cf_ref, q_ref, k_ref, v_ref = rest
    x = x_ref[0]
    ms = jnp.mean(x * x, axis=-1, keepdims=True)
    h = x * lax.rsqrt(ms + EPS) * g_ref[...]
    h = h * (1.0 + sc_ref[0]) + sh_ref[0]
    hb = h.astype(bf16)

    def proj(lo, hi):
        return _dot(hb, w_ref[:, lo:hi])

    hy_ref[0] = proj(0, OFF_CF).astype(bf16)
    cf_ref[0] = proj(OFF_CF, OFF_Q).astype(bf16)
    v_ref[0] = proj(OFF_V, IN_W).astype(bf16)

    lane = lax.broadcasted_iota(i32, (1, DA_QK), 1)
    first_half = (lane % DA_HD) < (DA_HD // 2)

    def qk_heads(lo, hi, gain_ref, out_ref, scale):
        t = proj(lo, hi)
        sq_hi, sq_lo = _split_bf16(t * t)
        ss = _dot(sq_hi, bd_ref[...]) + _dot(sq_lo, bd_ref[...])
        tn = t * lax.rsqrt(ss * (1.0 / DA_HD) + EPS) * gain_ref[...]
        if rope:
            half = DA_HD // 2
            partner = jnp.where(first_half, pltpu.roll(tn, DA_QK - half, 1), pltpu.roll(tn, half, 1))
            tn = tn * cos_ref[...] + partner * sin_ref[...]
        out_ref[0] = (tn * scale).astype(bf16)

    qk_heads(OFF_Q, OFF_K, qg_ref, q_ref, DA_HD ** -0.5)
    qk_heads(OFF_K, OFF_V, kg_ref, k_ref, 1.0)


def _inproj(x, shift, scale, g, w_bf, qg, kg, bd, rope_tabs):
    b, l, d = x.shape
    tm = min(512, l)
    rope = rope_tabs is not None
    row = lambda bi, i: (bi, i, 0)
    per_b = lambda bi, i: (bi, 0, 0)
    const = lambda bi, i: (0, 0)
    in_specs = [pl.BlockSpec((1, tm, d), row),
                pl.BlockSpec((1, 1, d), per_b),
                pl.BlockSpec((1, 1, d), per_b),
                pl.BlockSpec((1, d), const),
                pl.BlockSpec((d, IN_W), const),
                pl.BlockSpec((1, DA_QK), const),
                pl.BlockSpec((1, DA_QK), const),
                pl.BlockSpec((DA_QK, DA_QK), const)]
    args = [x, shift, scale, g, w_bf, qg, kg, bd]
    if rope:
        in_specs += [pl.BlockSpec((tm, DA_QK), lambda bi, i: (i, 0))] * 2
        args += list(rope_tabs)
    widths = (HY_IN, CF_IN, DA_QK, DA_QK, DA_W)
    return pl.pallas_call(
        functools.partial(_inproj_body, rope),
        grid=(b, l // tm),
        in_specs=in_specs,
        out_specs=[pl.BlockSpec((1, tm, w), row) for w in widths],
        out_shape=[jax.ShapeDtypeStruct((b, l, w), bf16) for w in widths],
        compiler_params=_params("parallel", "parallel"),
        name="inproj_rope" if rope else "inproj_ctx",
    )(*args)


def _hyprep_body(u_ref, w_ref, b_ref, x0_ref, z_ref):
    l = u_ref.shape[1]
    t = min(128, l)
    n = l // t
    row = lax.broadcasted_iota(i32, (t, 1), 0)

    def chunk(i, carry):
        t0 = pl.multiple_of(i * t, t)
        tp = pl.multiple_of(jnp.maximum(t0 - BF16_ROWS, 0), BF16_ROWS)
        tx = pl.multiple_of(jnp.minimum(t0 + t, l - BF16_ROWS), BF16_ROWS)
        outs = []
        for gi in range(3):
            ls = slice(gi * HY_W, (gi + 1) * HY_W)
            a = u_ref[0, pl.ds(t0, t), ls].astype(f32)
            prev = u_ref[0, pl.ds(tp, BF16_ROWS), ls].astype(f32)[BF16_ROWS - 1:BF16_ROWS]
            prev = jnp.where(i > 0, prev, 0.0)
            nxt = u_ref[0, pl.ds(tx, BF16_ROWS), ls].astype(f32)[0:1]
            nxt = jnp.where(i < n - 1, nxt, 0.0)
            um = jnp.where(row == 0, prev, pltpu.roll(a, 1, 0))
            up = jnp.where(row == t - 1, nxt, pltpu.roll(a, t - 1, 0))
            w = w_ref[:, ls]
            outs.append(w[0:1] * um + w[1:2] * a + w[2:3] * up + b_ref[:, ls])
        x0, x1, v = outs
        x0_ref[0, pl.ds(t0, t), :] = x0.astype(bf16)
        z_ref[0, pl.ds(t0, t), :] = (x1 * v).astype(bf16)
        return carry

    lax.fori_loop(0, n, chunk, 0)


def _hyprep(u_hy, conv_w, conv_b):
    b, l, _ = u_hy.shape
    per_b = lambda bi: (bi, 0, 0)
    const = lambda bi: (0, 0)
    return pl.pallas_call(
        _hyprep_body,
        grid=(b,),
        in_specs=[pl.BlockSpec((1, l, HY_IN), per_b),
                  pl.BlockSpec((3, HY_IN), const),
                  pl.BlockSpec((1, HY_IN), const)],
        out_specs=[pl.BlockSpec((1, l, HY_W), per_b)] * 2,
        out_shape=[jax.ShapeDtypeStruct((b, l, HY_W), bf16)] * 2,
        compiler_params=_params("parallel"),
        name="hyena_prep",
    )(u_hy, conv_w, conv_b.reshape(1, HY_IN))


def _filter_body(feat_ref, w1_ref, b1_ref, w2_ref, b2_ref, w3_ref, fr_ref, dl_ref, hs_ref, hd_ref):
    tl = feat_ref.shape[0]
    feats = feat_ref[...]
    fr = fr_ref[...]
    hid = jnp.sin(fr * (_dot(feats, w1_ref[...]) + b1_ref[...]))
    hid = jnp.sin(fr * (_dot(hid, w2_ref[...]) + b2_ref[...]))
    h = _dot(hid, w3_ref[...])
    win = jnp.exp(-feats[:, 0:1] * dl_ref[...])
    fwd = h[:, :HY_W] * win
    bwd = h[:, HY_W:] * win
    row = pl.program_id(0) * tl + lax.broadcasted_iota(i32, (tl, 1), 0)
    bwd = jnp.where(row == 0, 0.0, bwd)
    hs_ref[...] = fwd + bwd
    hd_ref[...] = bwd - fwd


def _hy_filter(feats, w1p, b1, w2, b2, w3, freq, deltas):
    l = feats.shape[0]
    tl = min(512, l)
    const = lambda i: (0, 0)
    return pl.pallas_call(
        _filter_body,
        grid=(l // tl,),
        in_specs=[pl.BlockSpec((tl, HY_EMB_PAD), lambda i: (i, 0)),
                  pl.BlockSpec((HY_EMB_PAD, HY_FF), const),
                  pl.BlockSpec((1, HY_FF), const),
                  pl.BlockSpec((HY_FF, HY_FF), const),
                  pl.BlockSpec((1, HY_FF), const),
                  pl.BlockSpec((HY_FF, 2 * HY_W), const),
                  pl.BlockSpec((1, HY_FF), const),
                  pl.BlockSpec((1, HY_W), const)],
        out_specs=[pl.BlockSpec((tl, HY_W), lambda i: (i, 0))] * 2,
        out_shape=[jax.ShapeDtypeStruct((l, HY_W), f32)] * 2,
        compiler_params=_params("parallel"),
        name="hyena_filter",
    )(feats, w1p, b1.reshape(1, HY_FF), w2, b2.reshape(1, HY_FF), w3, freq.reshape(1, HY_FF), deltas)


def _kf_body(inv_n, c_ref, s_ref, hs_ref, hd_ref, a_ref, b_ref, d_ref):
    tf = c_ref.shape[0]
    l = hs_ref.shape[0]
    hs = hs_ref[...]
    hs_hi, hs_lo = _split_bf16(hs)
    hd_hi, hd_lo = _split_bf16(hd_ref[...])
    kre = _dot(c_ref[...], hs_hi) + _dot(c_ref[...], hs_lo)
    kim = _dot(s_ref[...], hd_hi) + _dot(s_ref[...], hd_lo)
    tpos = lax.broadcasted_iota(i32, (l, 1), 0)
    knyq = jnp.sum(jnp.where(tpos % 2 == 0, hs, -hs), axis=0, keepdims=True)
    row = pl.program_id(0) * tf + lax.broadcasted_iota(i32, (tf, 1), 0)
    dc = row == 0
    a_ref[...] = jnp.where(dc, kre * inv_n, kre * (2.0 * inv_n))
    b_ref[...] = jnp.where(dc, 0.0, kim * (2.0 * inv_n))
    d_ref[...] = jnp.where(dc, knyq * inv_n, kre * (2.0 * inv_n))


def _hy_spectrum(cm, sm, hs, hd):
    l = hs.shape[0]
    tf = min(512, l)
    full = lambda i: (0, 0)
    rows = lambda i: (i, 0)
    return pl.pallas_call(
        functools.partial(_kf_body, 1.0 / (2 * l)),
        grid=(l // tf,),
        in_specs=[pl.BlockSpec((tf, l), rows), pl.BlockSpec((tf, l), rows),
                  pl.BlockSpec((l, HY_W), full), pl.BlockSpec((l, HY_W), full)],
        out_specs=[pl.BlockSpec((tf, HY_W), rows)] * 3,
        out_shape=[jax.ShapeDtypeStruct((l, HY_W), f32)] * 3,
        compiler_params=_params("parallel"),
        name="hyena_spectrum",
    )(cm, sm, hs, hd)


def _fftconv_body(z_ref, x0_ref, cr_ref, sr_ref, cc_ref, sc_ref, a_ref, b_ref, d_ref, bias_ref, o_ref, acc_ref):
    j = pl.program_id(1)
    z = z_ref[0]
    zre = _dot(cr_ref[...], z)
    zs = _dot(sr_ref[...], z)
    bco = b_ref[...]
    yre = (a_ref[...] * zre + bco * zs).astype(bf16)
    yim = (d_ref[...] * zs - bco * zre).astype(bf16)
    part = _dot(cc_ref[...], yre) + _dot(sc_ref[...], yim)

    @pl.when(j == 0)
    def _():
        acc_ref[...] = part

    @pl.when(j > 0)
    def _():
        acc_ref[...] += part

    @pl.when(j == pl.num_programs(1) - 1)
    def _():
        y = acc_ref[...] + bias_ref[...] * z.astype(f32)
        o_ref[0] = (x0_ref[0].astype(f32) * y).astype(bf16)


def _fftconv(z, x0, cm, sm, smt, a, bco, dco, bias):
    b, l, _ = z.shape
    tf = min(256, l)
    per_b = lambda bi, j: (bi, 0, 0)
    frow = lambda bi, j: (j, 0)
    fcol = lambda bi, j: (0, j)
    return pl.pallas_call(
        _fftconv_body,
        grid=(b, l // tf),
        in_specs=[pl.BlockSpec((1, l, HY_W), per_b), pl.BlockSpec((1, l, HY_W), per_b),
                  pl.BlockSpec((tf, l), frow), pl.BlockSpec((tf, l), frow),
                  pl.BlockSpec((l, tf), fcol), pl.BlockSpec((l, tf), fcol),
                  pl.BlockSpec((tf, HY_W), frow), pl.BlockSpec((tf, HY_W), frow), pl.BlockSpec((tf, HY_W), frow),
                  pl.BlockSpec((1, HY_W), lambda bi, j: (0, 0))],
        out_specs=pl.BlockSpec((1, l, HY_W), per_b),
        out_shape=jax.ShapeDtypeStruct((b, l, HY_W), bf16),
        scratch_shapes=[pltpu.VMEM((l, HY_W), f32)],
        compiler_params=_params("parallel", "arbitrary"),
        name="hyena_fftconv",
    )(z, x0, cm, sm, cm, smt, a, bco, dco, bias.reshape(1, HY_W))


def _conformer_body(u_ref, up_ref, un_ref, w_ref, b_ref, g_ref, beta_ref, o_ref, ext_ref):
    i = pl.program_id(1)
    tc = u_ref.shape[1]
    t = min(128, tc)

    def glu(ref, rows):
        a = ref[0, rows, 0:CF_W].astype(f32)
        g = ref[0, rows, CF_W:CF_IN].astype(f32)
        return a * jax.nn.sigmoid(g)

    ext_ref[0:CF_HALO, :] = jnp.where(i > 0, glu(up_ref, slice(None)), 0.0)
    ext_ref[CF_HALO + tc:CF_HALO + tc + CF_HALO, :] = jnp.where(i < pl.num_programs(1) - 1, glu(un_ref, slice(None)), 0.0)
    for t0 in range(0, tc, t):
        ext_ref[CF_HALO + t0:CF_HALO + t0 + t, :] = glu(u_ref, slice(t0, t0 + t))

    for t0 in range(0, tc, t):
        acc = jnp.zeros((t, CF_W), f32) + b_ref[...]
        for j in range(CF_CONV):
            lo = t0 + CF_HALO - CF_CONV // 2 + j
            acc = acc + w_ref[j:j + 1, :] * ext_ref[lo:lo + t, :]
        mu = jnp.mean(acc, axis=-1, keepdims=True)
        xc = acc - mu
        var = jnp.mean(xc * xc, axis=-1, keepdims=True)
        y = xc * lax.rsqrt(var + EPS) * g_ref[...] + beta_ref[...]
        o_ref[0, t0:t0 + t, :] = (y * jax.nn.sigmoid(y)).astype(bf16)


def _conformer(u_cf, dw_w, dw_b, ln_g, ln_b):
    b, l, _ = u_cf.shape
    tc = min(512, l)
    hb = tc // CF_HALO
    nh = l // CF_HALO
    const = lambda bi, i: (0, 0)
    vec = pl.BlockSpec((1, CF_W), const)
    return pl.pallas_call(
        _conformer_body,
        grid=(b, l // tc),
        in_specs=[pl.BlockSpec((1, tc, CF_IN), lambda bi, i: (bi, i, 0)),
                  pl.BlockSpec((1, CF_HALO, CF_IN), lambda bi, i: (bi, jnp.maximum(i * hb - 1, 0), 0)),
                  pl.BlockSpec((1, CF_HALO, CF_IN), lambda bi, i: (bi, jnp.minimum((i + 1) * hb, nh - 1), 0)),
                  pl.BlockSpec((CF_CONV, CF_W), const), vec, vec, vec],
        out_specs=pl.BlockSpec((1, tc, CF_W), lambda bi, i: (bi, i, 0)),
        out_shape=jax.ShapeDtypeStruct((b, l, CF_W), bf16),
        scratch_shapes=[pltpu.VMEM((tc + 2 * CF_HALO, CF_W), f32)],
        compiler_params=_params("parallel", "parallel"),
        name="conformer_conv",
    )(u_cf, u_cf, u_cf, dw_w, dw_b.reshape(1, CF_W), ln_g.reshape(1, CF_W), ln_b.reshape(1, CF_W))


def _attn_body(nseg, lam_init, lam_ref, g_ref, q_ref, *rest):
    kv = rest[:2 * nseg]
    o_ref = rest[2 * nseg]
    q = q_ref[0]
    lane = lax.broadcasted_iota(i32, (1, DA_VD), 1)
    lp = lam_ref[...]
    lam = (jnp.exp(jnp.sum(lp[0:1] * lp[1:2], axis=-1, keepdims=True))
           - jnp.exp(jnp.sum(lp[2:3] * lp[3:4], axis=-1, keepdims=True)) + lam_init)
    probs, coef = [], []
    for m in range(2):
        qm = jnp.where((lane >= m * DA_HD) & (lane < (m + 1) * DA_HD), q, jnp.zeros_like(q))
        s = [_dot_nt(qm, kv[2 * i][0]) for i in range(nseg)]
        mx = functools.reduce(jnp.maximum, [jnp.max(si, axis=-1, keepdims=True) for si in s])
        p = [jnp.exp(si - mx) for si in s]
        den = functools.reduce(lambda u, v: u + v, [jnp.sum(pi, axis=-1, keepdims=True) for pi in p])
        probs.append(p)
        coef.append(1.0 / den)
    c0 = coef[0]
    c1 = coef[1] * lam
    o = None
    for i in range(nseg):
        a = (probs[0][i] * c0 - probs[1][i] * c1).astype(bf16)
        part = _dot(a, kv[2 * i + 1][0])
        o = part if o is None else o + part
    ms = jnp.mean(o * o, axis=-1, keepdims=True)
    o_ref[0] = (o * lax.rsqrt(ms + EPS) * g_ref[...] * (1.0 - lam_init)).astype(bf16)


def _diff_attn(q, kvs, lam_p, subln_g, lam_init):
    b, lq, _ = q.shape
    tq = min(256, lq)
    nseg = len(kvs)
    in_specs = [pl.BlockSpec((4, DA_HD), lambda bi, h, i: (0, 0)),
                pl.BlockSpec((1, DA_VD), lambda bi, h, i: (0, 0)),
                pl.BlockSpec((1, tq, DA_VD), lambda bi, h, i: (bi, i, h))]
    args = [lam_p, subln_g.reshape(1, DA_VD), q]
    for k, v in kvs:
        lk = k.shape[1]
        in_specs += [pl.BlockSpec((1, lk, DA_VD), lambda bi, h, i: (bi, 0, h))] * 2
        args += [k, v]
    return pl.pallas_call(
        functools.partial(_attn_body, nseg, lam_init),
        grid=(b, DA_HEADS, lq // tq),
        in_specs=in_specs,
        out_specs=pl.BlockSpec((1, tq, DA_VD), lambda bi, h, i: (bi, i, h)),
        out_shape=jax.ShapeDtypeStruct((b, lq, DA_W), bf16),
        compiler_params=_params("parallel", "parallel", "parallel"),
        name=f"diff_attn_{nseg}seg",
    )(*args)


def _outproj_body(hy_ref, cf_ref, da_ref, w_ref, x_ref, g2_ref, gn_ref, sh_ref, sc_ref, wr_hi_ref, wr_lo_ref, br_ref,
                  xo_ref, h2_ref, gates_ref):
    y = (_dot(hy_ref[0], w_ref[0:HY_W, :]) + _dot(cf_ref[0], w_ref[HY_W:HY_W + CF_W, :])
         + _dot(da_ref[0], w_ref[HY_W + CF_W:MIX_W, :]))
    x = x_ref[0] + g2_ref[0] * y
    xo_ref[0] = x
    ms = jnp.mean(x * x, axis=-1, keepdims=True)
    h = x * lax.rsqrt(ms + EPS) * gn_ref[...]
    h = h * (1.0 + sc_ref[0]) + sh_ref[0]
    h2_ref[0] = h.astype(bf16)

    h_hi, h_lo = _split_bf16(h)
    logits = _dot(h_hi, wr_hi_ref[...]) + _dot(h_lo, wr_hi_ref[...]) + _dot(h_hi, wr_lo_ref[...])
    scores = jax.nn.sigmoid(logits)
    sel = scores + br_ref[...]
    tm = sel.shape[0]
    e = lax.broadcasted_iota(i32, (tm, N_EXPERTS), 1)
    grp = e // EXPERTS_PER_GROUP
    neg = -jnp.inf

    def top2(vals):
        m1 = jnp.max(vals, axis=-1, keepdims=True)
        i1 = jnp.min(jnp.where(vals == m1, e, N_EXPERTS), axis=-1, keepdims=True)
        rest = jnp.where(e == i1, neg, vals)
        m2 = jnp.max(rest, axis=-1, keepdims=True)
        i2 = jnp.min(jnp.where(rest == m2, e, N_EXPERTS), axis=-1, keepdims=True)
        return m1, i1, m2, i2

    best_score = None
    best_grp = None
    for gidx in range(N_GROUPS):
        m1, _, m2, _ = top2(jnp.where(grp == gidx, sel, neg))
        gs = m1 + m2
        if best_score is None:
            best_score, best_grp = gs, jnp.zeros_like(gs, dtype=i32)
        else:
            better = gs > best_score
            best_score = jnp.where(better, gs, best_score)
            best_grp = jnp.where(better, gidx, best_grp)
    _, i1, _, i2 = top2(jnp.where(grp == best_grp, sel, neg))
    s1 = jnp.sum(jnp.where(e == i1, scores, 0.0), axis=-1, keepdims=True)
    s2 = jnp.sum(jnp.where(e == i2, scores, 0.0), axis=-1, keepdims=True)
    inv = 1.0 / (s1 + s2)
    gates_ref[0] = jnp.where(e == i1, s1 * inv, 0.0) + jnp.where(e == i2, s2 * inv, 0.0)


def _outproj_router(hy, cf, da, w_out_bf, x, gate2, norm2_g, shift2, scale2, wr_hi, wr_lo, b_router):
    b, l, d = x.shape
    tm = min(512, l)
    row = lambda bi, i: (bi, i, 0)
    per_b = lambda bi, i: (bi, 0, 0)
    const = lambda bi, i: (0, 0)
    mod = pl.BlockSpec((1, 1, d), per_b)
    return pl.pallas_call(
        _outproj_body,
        grid=(b, l // tm),
        in_specs=[pl.BlockSpec((1, tm, HY_W), row), pl.BlockSpec((1, tm, CF_W), row), pl.BlockSpec((1, tm, DA_W), row),
                  pl.BlockSpec((MIX_W, d), const), pl.BlockSpec((1, tm, d), row), mod,
                  pl.BlockSpec((1, d), const), mod, mod,
                  pl.BlockSpec((d, N_EXPERTS), const), pl.BlockSpec((d, N_EXPERTS), const),
                  pl.BlockSpec((1, N_EXPERTS), const)],
        out_specs=[pl.BlockSpec((1, tm, d), row), pl.BlockSpec((1, tm, d), row), pl.BlockSpec((1, tm, N_EXPERTS), row)],
        out_shape=[jax.ShapeDtypeStruct((b, l, d), f32), jax.ShapeDtypeStruct((b, l, d), bf16),
                   jax.ShapeDtypeStruct((b, l, N_EXPERTS), f32)],
        compiler_params=_params("parallel", "parallel"),
        name="outproj_router",
    )(hy, cf, da, w_out_bf, x, gate2, norm2_g, shift2, scale2, wr_hi, wr_lo, b_router)


def _moe_body(h_ref, gates_ref, x_ref, g5_ref, w1_ref, w3_ref, w2_ref, o_ref, acc_ref):
    ex = pl.program_id(2)
    h = h_ref[0]
    a = _dot(h, w1_ref[0])
    a = a * jax.nn.sigmoid(a) * _dot(h, w3_ref[0])
    y = _dot(a.astype(bf16), w2_ref[0])
    gates = gates_ref[0]
    e = lax.broadcasted_iota(i32, gates.shape, 1)
    gate = jnp.sum(jnp.where(e == ex, gates, 0.0), axis=-1, keepdims=True)

    @pl.when(ex == 0)
    def _():
        acc_ref[...] = gate * y

    @pl.when(ex > 0)
    def _():
        acc_ref[...] += gate * y

    @pl.when(ex == pl.num_programs(2) - 1)
    def _():
        o_ref[0] = x_ref[0] + g5_ref[0] * acc_ref[...]


def _moe(h2, gates, x, gate5, w1_bf, w3_bf, w2_bf):
    b, l, d = x.shape
    tm = min(512, l)
    row = lambda bi, i, ex: (bi, i, 0)
    wmap = lambda bi, i, ex: (ex, 0, 0)
    return pl.pallas_call(
        _moe_body,
        grid=(b, l // tm, N_EXPERTS),
        in_specs=[pl.BlockSpec((1, tm, d), row), pl.BlockSpec((1, tm, N_EXPERTS), row), pl.BlockSpec((1, tm, d), row),
                  pl.BlockSpec((1, 1, d), lambda bi, i, ex: (bi, 0, 0)),
                  pl.BlockSpec((1, d, D_EXPERT), wmap), pl.BlockSpec((1, d, D_EXPERT), wmap),
                  pl.BlockSpec((1, D_EXPERT, d), wmap)],
        out_specs=pl.BlockSpec((1, tm, d), row),
        out_shape=jax.ShapeDtypeStruct((b, l, d), f32),
        scratch_shapes=[pltpu.VMEM((tm, d), f32)],
        compiler_params=_params("parallel", "parallel", "arbitrary"),
        name="moe_dense",
    )(h2, gates, x, gate5, w1_bf, w3_bf, w2_bf)


def _rope_tables(n_lat):
    rows = n_lat // GRID_W
    row = jnp.repeat(jnp.arange(rows, dtype=f32), GRID_W)
    col = jnp.tile(jnp.arange(GRID_W, dtype=f32), rows)
    inv = ROPE_THETA ** (-jnp.arange(ROPE_AXIS_FREQS, dtype=f32) / ROPE_AXIS_FREQS)
    ang = jnp.concatenate([row[:, None] * inv, col[:, None] * inv], axis=-1)
    cos, sin = jnp.cos(ang), jnp.sin(ang)
    reps = DA_QK // DA_HD
    return jnp.tile(jnp.concatenate([cos, cos], -1), (1, reps)), jnp.tile(jnp.concatenate([-sin, sin], -1), (1, reps))


def _dft_tables(l):
    n = 2 * l
    f = jnp.arange(l, dtype=i32)[:, None]
    t = jnp.arange(l, dtype=i32)[None, :]
    ang = ((f * t) % n).astype(f32) * (2.0 * math.pi / n)
    cm = jnp.cos(ang)
    sm = jnp.where(f == 0, jnp.where(t % 2 == 0, 1.0, -1.0), jnp.sin(ang))
    return cm.astype(bf16), sm.astype(bf16), sm.T.astype(bf16)


def _hy_features(l):
    t_idx = jnp.arange(l, dtype=f32)[:, None]
    t01 = t_idx / max(l - 1, 1)
    bands = jnp.linspace(1e-4, HY_BANDS - 1, HY_BANDS, dtype=f32)
    ang = 2.0 * math.pi * bands * t_idx / l
    feats = jnp.concatenate([t01, jnp.cos(ang), -jnp.sin(ang)], axis=-1)
    return jnp.pad(feats, ((0, 0), (0, HY_EMB_PAD - HY_EMB)))


def _hyena(u_hy, dft, feats, deltas, conv_w, conv_b, w1p, b1, w2, b2, w3, freq, bias):
    cm, sm, smt = dft
    x0, z = _hyprep(u_hy, conv_w, conv_b)
    hs, hd = _hy_filter(feats, w1p, b1, w2, b2, w3, freq, deltas)
    a, bco, dco = _hy_spectrum(cm, sm, hs, hd)
    return _fftconv(z, x0, cm, sm, smt, a, bco, dco, bias)


def kernel(x, c, ctx, c_ctx, w_ada, b_ada, norm1_g, norm2_g, w_in, hy_conv_w, hy_conv_b, hy_w1, hy_b1, hy_w2, hy_b2,
           hy_w3, hy_freq, hy_bias, cf_dw_w, cf_dw_b, cf_ln_g, cf_ln_b, da_qn_g, da_kn_g, da_lam, da_subln_g, w_out,
           w_router, b_router, moe_w1, moe_w3, moe_w2):
    depth = w_ada.shape[0]
    bsz, n_lat, d = x.shape
    n_ctx = ctx.shape[1]

    rope = _rope_tables(n_lat)
    dft_lat, dft_ctx = _dft_tables(n_lat), _dft_tables(n_ctx)
    feats_lat, feats_ctx = _hy_features(n_lat), _hy_features(n_ctx)
    deltas = jnp.abs(jnp.linspace(HY_MIN_DECAY, HY_MAX_DECAY, HY_W, dtype=f32)).reshape(1, HY_W)
    seg = jnp.arange(DA_QK, dtype=i32) // DA_HD
    bd = (seg[:, None] == seg[None, :]).astype(bf16)

    rows = jnp.concatenate([c, c_ctx[None], jnp.zeros((7, d), f32)], axis=0)
    mod_all = _ada(rows, w_ada, b_ada)

    wr_hi = w_router.astype(bf16)
    wr_lo = (w_router - wr_hi.astype(f32)).astype(bf16)
    br = b_router.reshape(1, N_EXPERTS)
    hy_w1p = jnp.pad(hy_w1, ((0, 0), (0, HY_EMB_PAD - HY_EMB), (0, 0)))

    x_lat, x_ctx = x, ctx
    for l in range(depth):
        last = l == depth - 1
        lam_init = 0.8 - 0.6 * math.exp(-0.3 * l)
        mod = mod_all[l, :bsz].reshape(bsz, 6, 1, d)
        mod_c = jnp.broadcast_to(mod_all[l, bsz].reshape(1, 6, 1, d), (bsz, 6, 1, d))
        w_in_bf = w_in[l].astype(bf16)
        w_out_bf = w_out[l].astype(bf16)
        w1_bf, w3_bf, w2_bf = moe_w1[l].astype(bf16), moe_w3[l].astype(bf16), moe_w2[l].astype(bf16)
        g1 = norm1_g[l].reshape(1, d)
        g2 = norm2_g[l].reshape(1, d)
        qg = jnp.tile(da_qn_g[l], DA_QK // DA_HD).reshape(1, DA_QK)
        kg = jnp.tile(da_kn_g[l], DA_QK // DA_HD).reshape(1, DA_QK)
        hy_p = (hy_conv_w[l], hy_conv_b[l], hy_w1p[l], hy_b1[l], hy_w2[l], hy_b2[l], hy_w3[l], hy_freq[l], hy_bias[l])
        cf_p = (cf_dw_w[l], cf_dw_b[l], cf_ln_g[l], cf_ln_b[l])

        uhy, ucf, q_lat, k_lat, v_lat = _inproj(x_lat, mod[:, 0], mod[:, 1], g1, w_in_bf, qg, kg, bd, rope)
        uhy_c, ucf_c, q_ctx, k_ctx, v_ctx = _inproj(x_ctx, mod_c[:, 0], mod_c[:, 1], g1, w_in_bf, qg, kg, bd, None)
        hy_lat = _hyena(uhy, dft_lat, feats_lat, deltas, *hy_p)
        cf_lat = _conformer(ucf, *cf_p)
        da_lat = _diff_attn(q_lat, [(k_ctx, v_ctx), (k_lat, v_lat)], da_lam[l], da_subln_g[l], lam_init)
        x_lat, h2, gates = _outproj_router(hy_lat, cf_lat, da_lat, w_out_bf, x_lat, mod[:, 2], g2, mod[:, 3], mod[:, 4],
                                           wr_hi, wr_lo, br)
        x_lat = _moe(h2, gates, x_lat, mod[:, 5], w1_bf, w3_bf, w2_bf)
        if not last:
            hy_c = _hyena(uhy_c, dft_ctx, feats_ctx, deltas, *hy_p)
            cf_c = _conformer(ucf_c, *cf_p)
            da_c = _diff_attn(q_ctx, [(k_ctx, v_ctx)], da_lam[l], da_subln_g[l], lam_init)
            x_ctx, h2c, gates_c = _outproj_router(hy_c, cf_c, da_c, w_out_bf, x_ctx, mod_c[:, 2], g2, mod_c[:, 3],
                                                  mod_c[:, 4], wr_hi, wr_lo, br)
            x_ctx = _moe(h2c, gates_c, x_ctx, mod_c[:, 5], w1_bf, w3_bf, w2_bf)
    return x_lat
```

```python
import functools
import math

import jax
import jax.numpy as jnp
from jax import lax
from jax.experimental import pallas as pl
from jax.experimental.pallas import tpu as pltpu

f32 = jnp.float32
bf16 = jnp.bfloat16
i32 = jnp.int32

D_MODEL = 1024
GRID_W = 64
HY_W = 256
CF_W = 256
DA_HEADS = 4
DA_HD = 64
DA_VD = 2 * DA_HD
DA_W = DA_HEADS * DA_VD
MIX_W = HY_W + CF_W + DA_W
HY_IN = 3 * HY_W
CF_IN = 2 * CF_W
DA_QK = DA_HEADS * 2 * DA_HD
OFF_CF = HY_IN
OFF_Q = OFF_CF + CF_IN
OFF_K = OFF_Q + DA_QK
OFF_V = OFF_K + DA_QK
IN_W = OFF_V + DA_W
CF_CONV = 31
HY_BANDS = 16
HY_EMB = 1 + 2 * HY_BANDS
HY_EMB_PAD = 64
HY_FF = 64
HY_MIN_DECAY = math.log(1e-2) / 1.5
HY_MAX_DECAY = math.log(1e-2) / 0.3
ROPE_THETA = 10000.0
ROPE_AXIS_FREQS = DA_HD // 4
N_EXPERTS = 16
N_GROUPS = 4
EXPERTS_PER_GROUP = N_EXPERTS // N_GROUPS
D_EXPERT = 512
EPS = 1e-6

VMEM_LIMIT_BYTES = 56 * 1024 * 1024
CF_HALO = 16
BF16_ROWS = 16
ATT_TQ = 256
ATT_SLAB = 64
ATT_NSUB = 4
ATT_CK = 512
LOG2E = 1.4426950408889634
MOE_TAIL = 128
MOE_ROW = D_MODEL + MOE_TAIL


def _params(*sem):
    return pltpu.CompilerParams(dimension_semantics=sem, vmem_limit_bytes=VMEM_LIMIT_BYTES)


def _dot(a, b):
    return jnp.dot(a, b, preferred_element_type=f32)


def _dot_nt(a, b):
    return lax.dot_general(a, b, (((1,), (1,)), ((), ())), preferred_element_type=f32)


def _split_bf16(v):
    hi = v.astype(bf16)
    lo = (v - hi.astype(f32)).astype(bf16)
    return hi, lo


def _ada_body(c_ref, w_ref, b_ref, o_ref):
    c = c_ref[...]
    s = c * jax.nn.sigmoid(c)
    o_ref[0] = _dot(s, w_ref[0]) + b_ref[0]


def _ada(cc, w_ada, b_ada):
    depth, d, n = w_ada.shape
    r = cc.shape[0]
    tn = 1536
    return pl.pallas_call(
        _ada_body,
        grid=(depth, n // tn),
        in_specs=[pl.BlockSpec((r, d), lambda l, j: (0, 0)),
                  pl.BlockSpec((1, d, tn), lambda l, j: (l, 0, j)),
                  pl.BlockSpec((1, 1, tn), lambda l, j: (l, 0, j))],
        out_specs=pl.BlockSpec((1, r, tn), lambda l, j: (l, 0, j)),
        out_shape=jax.ShapeDtypeStruct((depth, r, n), f32),
        compiler_params=_params("parallel", "parallel"),
        name="ada_mod",
    )(cc, w_ada, b_ada.reshape(depth, 1, n))


def _inproj_body(rope, x_ref, sh_ref, sc_ref, g_ref, w_ref, qg_ref, kg_ref, bd_ref, *rest):
    if rope:
        cos_ref, sin_ref, hy_ref, cf_ref, q_ref, k_ref, v_ref = rest
    else:
        hy_ref, cf_ref, q_ref, k_ref, v_ref = rest
    x = x_ref[0]
    ms = jnp.mean(x * x, axis=-1, keepdims=True)
    h = x * lax.rsqrt(ms + EPS) * g_ref[...]
    h = h * (1.0 + sc_ref[0]) + sh_ref[0]
    hb = h.astype(bf16)

    def proj(lo, hi):
        return _dot(hb, w_ref[:, lo:hi])

    hy_ref[0] = proj(0, OFF_CF).astype(bf16)
    cf_ref[0] = proj(OFF_CF, OFF_Q).astype(bf16)
    v_ref[0] = proj(OFF_V, IN_W).astype(bf16)

    lane = lax.broadcasted_iota(i32, (1, DA_QK), 1)
    first_half = (lane % DA_HD) < (DA_HD // 2)

    def qk_heads(lo, hi, gain_ref, out_ref, scale):
        t = proj(lo, hi)
        sq_hi, sq_lo = _split_bf16(t * t)
        ss = _dot(sq_hi, bd_ref[...]) + _dot(sq_lo, bd_ref[...])
        tn = t * lax.rsqrt(ss * (1.0 / DA_HD) + EPS) * gain_ref[...]
        if rope:
            half = DA_HD // 2
            partner = jnp.where(first_half, pltpu.roll(tn, DA_QK - half, 1), pltpu.roll(tn, half, 1))
            tn = tn * cos_ref[...] + partner * sin_ref[...]
        out_ref[0] = (tn * scale).astype(bf16)

    qk_heads(OFF_Q, OFF_K, qg_ref, q_ref, LOG2E * DA_HD ** -0.5)
    qk_heads(OFF_K, OFF_V, kg_ref, k_ref, 1.0)


def _inproj(x, shift, scale, g, w_bf, qg, kg, bd, rope_tabs):
    b, l, d = x.shape
    tm = min(512, l)
    rope = rope_tabs is not None
    row = lambda bi, i: (bi, i, 0)
    per_b = lambda bi, i: (bi, 0, 0)
    const = lambda bi, i: (0, 0)
    in_specs = [pl.BlockSpec((1, tm, d), row),
                pl.BlockSpec((1, 1, d), per_b),
                pl.BlockSpec((1, 1, d), per_b),
                pl.BlockSpec((1, d), const),
                pl.BlockSpec((d, IN_W), const),
                pl.BlockSpec((1, DA_QK), const),
                pl.BlockSpec((1, DA_QK), const),
                pl.BlockSpec((DA_QK, DA_QK), const)]
    args = [x, shift, scale, g, w_bf, qg, kg, bd]
    if rope:
        in_specs += [pl.BlockSpec((tm, DA_QK), lambda bi, i: (i, 0))] * 2
        args += list(rope_tabs)
    widths = (HY_IN, CF_IN, DA_QK, DA_QK, DA_W)
    return pl.pallas_call(
        functools.partial(_inproj_body, rope),
        grid=(b, l // tm),
        in_specs=in_specs,
        out_specs=[pl.BlockSpec((1, tm, w), row) for w in widths],
        out_shape=[jax.ShapeDtypeStruct((b, l, w), bf16) for w in widths],
        compiler_params=_params("parallel", "parallel"),
        name="inproj_rope" if rope else "inproj_ctx",
    )(*args)


def _hyprep_body(u_ref, w_ref, b_ref, x0_ref, z_ref):
    l = u_ref.shape[1]
    t = min(128, l)
    n = l // t
    row = lax.broadcasted_iota(i32, (t, 1), 0)

    def chunk(i, carry):
        t0 = pl.multiple_of(i * t, t)
        tp = pl.multiple_of(jnp.maximum(t0 - BF16_ROWS, 0), BF16_ROWS)
        tx = pl.multiple_of(jnp.minimum(t0 + t, l - BF16_ROWS), BF16_ROWS)
        outs = []
        for gi in range(3):
            ls = slice(gi * HY_W, (gi + 1) * HY_W)
            a = u_ref[0, pl.ds(t0, t), ls].astype(f32)
            prev = u_ref[0, pl.ds(tp, BF16_ROWS), ls].astype(f32)[BF16_ROWS - 1:BF16_ROWS]
            prev = jnp.where(i > 0, prev, 0.0)
            nxt = u_ref[0, pl.ds(tx, BF16_ROWS), ls].astype(f32)[0:1]
            nxt = jnp.where(i < n - 1, nxt, 0.0)
            um = jnp.where(row == 0, prev, pltpu.roll(a, 1, 0))
            up = jnp.where(row == t - 1, nxt, pltpu.roll(a, t - 1, 0))
            w = w_ref[:, ls]
            outs.append(w[0:1] * um + w[1:2] * a + w[2:3] * up + b_ref[:, ls])
        x0, x1, v = outs
        x0_ref[0, pl.ds(t0, t), :] = x0.astype(bf16)
        z_ref[0, pl.ds(t0, t), :] = (x1 * v).astype(bf16)
        return carry

    lax.fori_loop(0, n, chunk, 0)


def _hyprep(u_hy, conv_w, conv_b):
    b, l, _ = u_hy.shape
    per_b = lambda bi: (bi, 0, 0)
    const = lambda bi: (0, 0)
    return pl.pallas_call(
        _hyprep_body,
        grid=(b,),
        in_specs=[pl.BlockSpec((1, l, HY_IN), per_b),
                  pl.BlockSpec((3, HY_IN), const),
                  pl.BlockSpec((1, HY_IN), const)],
        out_specs=[pl.BlockSpec((1, l, HY_W), per_b)] * 2,
        out_shape=[jax.ShapeDtypeStruct((b, l, HY_W), bf16)] * 2,
        compiler_params=_params("parallel"),
        name="hyena_prep",
    )(u_hy, conv_w, conv_b.reshape(1, HY_IN))


def _filter_body(feat_ref, w1_ref, b1_ref, w2_ref, b2_ref, w3_ref, fr_ref, dl_ref, hs_ref, hd_ref):
    tl = feat_ref.shape[0]
    feats = feat_ref[...]
    fr = fr_ref[...]
    hid = jnp.sin(fr * (_dot(feats, w1_ref[...]) + b1_ref[...]))
    hid = jnp.sin(fr * (_dot(hid, w2_ref[...]) + b2_ref[...]))
    h = _dot(hid, w3_ref[...])
    win = jnp.exp(-feats[:, 0:1] * dl_ref[...])
    fwd = h[:, :HY_W] * win
    bwd = h[:, HY_W:] * win
    row = pl.program_id(0) * tl + lax.broadcasted_iota(i32, (tl, 1), 0)
    bwd = jnp.where(row == 0, 0.0, bwd)
    hs_ref[...] = fwd + bwd
    hd_ref[...] = bwd - fwd


def _hy_filter(feats, w1p, b1, w2, b2, w3, freq, deltas):
    l = feats.shape[0]
    tl = min(512, l)
    const = lambda i: (0, 0)
    return pl.pallas_call(
        _filter_body,
        grid=(l // tl,),
        in_specs=[pl.BlockSpec((tl, HY_EMB_PAD), lambda i: (i, 0)),
                  pl.BlockSpec((HY_EMB_PAD, HY_FF), const),
                  pl.BlockSpec((1, HY_FF), const),
                  pl.BlockSpec((HY_FF, HY_FF), const),
                  pl.BlockSpec((1, HY_FF), const),
                  pl.BlockSpec((HY_FF, 2 * HY_W), const),
                  pl.BlockSpec((1, HY_FF), const),
                  pl.BlockSpec((1, HY_W), const)],
        out_specs=[pl.BlockSpec((tl, HY_W), lambda i: (i, 0))] * 2,
        out_shape=[jax.ShapeDtypeStruct((l, HY_W), f32)] * 2,
        compiler_params=_params("parallel"),
        name="hyena_filter",
    )(feats, w1p, b1.reshape(1, HY_FF), w2, b2.reshape(1, HY_FF), w3, freq.reshape(1, HY_FF), deltas)


def _kf_body(inv_n, c_ref, s_ref, hs_ref, hd_ref, a_ref, b_ref, d_ref):
    tf = c_ref.shape[0]
    l = hs_ref.shape[0]
    hs = hs_ref[...]
    hs_hi, hs_lo = _split_bf16(hs)
    hd_hi, hd_lo = _split_bf16(hd_ref[...])
    kre = _dot(c_ref[...], hs_hi) + _dot(c_ref[...], hs_lo)
    kim = _dot(s_ref[...], hd_hi) + _dot(s_ref[...], hd_lo)
    tpos = lax.broadcasted_iota(i32, (l, 1), 0)
    knyq = jnp.sum(jnp.where(tpos % 2 == 0, hs, -hs), axis=0, keepdims=True)
    row = pl.program_id(0) * tf + lax.broadcasted_iota(i32, (tf, 1), 0)
    dc = row == 0
    a_ref[...] = jnp.where(dc, kre * inv_n, kre * (2.0 * inv_n))
    b_ref[...] = jnp.where(dc, 0.0, kim * (2.0 * inv_n))
    d_ref[...] = jnp.where(dc, knyq * inv_n, kre * (2.0 * inv_n))


def _hy_spectrum(cm, sm, hs, hd):
    l = hs.shape[0]
    tf = min(512, l)
    full = lambda i: (0, 0)
    rows = lambda i: (i, 0)
    return pl.pallas_call(
        functools.partial(_kf_body, 1.0 / (2 * l)),
        grid=(l // tf,),
        in_specs=[pl.BlockSpec((tf, l), rows), pl.BlockSpec((tf, l), rows),
                  pl.BlockSpec((l, HY_W), full), pl.BlockSpec((l, HY_W), full)],
        out_specs=[pl.BlockSpec((tf, HY_W), rows)] * 3,
        out_shape=[jax.ShapeDtypeStruct((l, HY_W), f32)] * 3,
        compiler_params=_params("parallel"),
        name="hyena_spectrum",
    )(cm, sm, hs, hd)


def _fftconv_body(z_ref, x0_ref, cr_ref, sr_ref, cc_ref, sc_ref, a_ref, b_ref, d_ref, bias_ref, o_ref, acc_ref):
    j = pl.program_id(1)
    z = z_ref[0]
    zre = _dot(cr_ref[...], z)
    zs = _dot(sr_ref[...], z)
    bco = b_ref[...]
    yre = (a_ref[...] * zre + bco * zs).astype(bf16)
    yim = (d_ref[...] * zs - bco * zre).astype(bf16)
    part = _dot(cc_ref[...], yre) + _dot(sc_ref[...], yim)

    @pl.when(j == 0)
    def _():
        acc_ref[...] = part

    @pl.when(j > 0)
    def _():
        acc_ref[...] += part

    @pl.when(j == pl.num_programs(1) - 1)
    def _():
        y = acc_ref[...] + bias_ref[...] * z.astype(f32)
        o_ref[0] = (x0_ref[0].astype(f32) * y).astype(bf16)


def _fftconv(z, x0, cm, sm, smt, a, bco, dco, bias):
    b, l, _ = z.shape
    tf = min(256, l)
    per_b = lambda bi, j: (bi, 0, 0)
    frow = lambda bi, j: (j, 0)
    fcol = lambda bi, j: (0, j)
    return pl.pallas_call(
        _fftconv_body,
        grid=(b, l // tf),
        in_specs=[pl.BlockSpec((1, l, HY_W), per_b), pl.BlockSpec((1, l, HY_W), per_b),
                  pl.BlockSpec((tf, l), frow), pl.BlockSpec((tf, l), frow),
                  pl.BlockSpec((l, tf), fcol), pl.BlockSpec((l, tf), fcol),
                  pl.BlockSpec((tf, HY_W), frow), pl.BlockSpec((tf, HY_W), frow), pl.BlockSpec((tf, HY_W), frow),
                  pl.BlockSpec((1, HY_W), lambda bi, j: (0, 0))],
        out_specs=pl.BlockSpec((1, l, HY_W), per_b),
        out_shape=jax.ShapeDtypeStruct((b, l, HY_W), bf16),
        scratch_shapes=[pltpu.VMEM((l, HY_W), f32)],
        compiler_params=_params("parallel", "arbitrary"),
        name="hyena_fftconv",
    )(z, x0, cm, sm, cm, smt, a, bco, dco, bias.reshape(1, HY_W))


def _conformer_body(u_ref, up_ref, un_ref, w_ref, b_ref, g_ref, beta_ref, o_ref, ext_ref):
    i = pl.program_id(1)
    tc = u_ref.shape[1]
    t = min(128, tc)

    def glu(ref, rows):
        a = ref[0, rows, 0:CF_W].astype(f32)
        g = ref[0, rows, CF_W:CF_IN].astype(f32)
        return a * jax.nn.sigmoid(g)

    ext_ref[0:CF_HALO, :] = jnp.where(i > 0, glu(up_ref, slice(None)), 0.0)
    ext_ref[CF_HALO + tc:CF_HALO + tc + CF_HALO, :] = jnp.where(i < pl.num_programs(1) - 1, glu(un_ref, slice(None)), 0.0)
    for t0 in range(0, tc, t):
        ext_ref[CF_HALO + t0:CF_HALO + t0 + t, :] = glu(u_ref, slice(t0, t0 + t))

    for t0 in range(0, tc, t):
        acc = jnp.zeros((t, CF_W), f32) + b_ref[...]
        for j in range(CF_CONV):
            lo = t0 + CF_HALO - CF_CONV // 2 + j
            acc = acc + w_ref[j:j + 1, :] * ext_ref[lo:lo + t, :]
        mu = jnp.mean(acc, axis=-1, keepdims=True)
        xc = acc - mu
        var = jnp.mean(xc * xc, axis=-1, keepdims=True)
        y = xc * lax.rsqrt(var + EPS) * g_ref[...] + beta_ref[...]
        o_ref[0, t0:t0 + t, :] = (y * jax.nn.sigmoid(y)).astype(bf16)


def _conformer(u_cf, dw_w, dw_b, ln_g, ln_b):
    b, l, _ = u_cf.shape
    tc = min(512, l)
    hb = tc // CF_HALO
    nh = l // CF_HALO
    const = lambda bi, i: (0, 0)
    vec = pl.BlockSpec((1, CF_W), const)
    return pl.pallas_call(
        _conformer_body,
        grid=(b, l // tc),
        in_specs=[pl.BlockSpec((1, tc, CF_IN), lambda bi, i: (bi, i, 0)),
                  pl.BlockSpec((1, CF_HALO, CF_IN), lambda bi, i: (bi, jnp.maximum(i * hb - 1, 0), 0)),
                  pl.BlockSpec((1, CF_HALO, CF_IN), lambda bi, i: (bi, jnp.minimum((i + 1) * hb, nh - 1), 0)),
                  pl.BlockSpec((CF_CONV, CF_W), const), vec, vec, vec],
        out_specs=pl.BlockSpec((1, tc, CF_W), lambda bi, i: (bi, i, 0)),
        out_shape=jax.ShapeDtypeStruct((b, l, CF_W), bf16),
        scratch_shapes=[pltpu.VMEM((tc + 2 * CF_HALO, CF_W), f32)],
        compiler_params=_params("parallel", "parallel"),
        name="conformer_conv",
    )(u_cf, u_cf, u_cf, dw_w, dw_b.reshape(1, CF_W), ln_g.reshape(1, CF_W), ln_b.reshape(1, CF_W))


def _attn_body(nsub, tq, ck, n_lat, lam_init, lam_ref, g_ref, q_ref, kc_ref, vc_ref, *rest):
    if n_lat:
        kl_ref, vl_ref = rest[:2]
        rest = rest[2:]
    o_ref, s_a, s_b, p_a, p_b, mpart_ref, mrow_ref, lpart_ref, c0_ref, r_ref, acc_ref = rest
    s_bufs, p_bufs = (s_a, s_b), (p_a, p_b)
    n_ctx = kc_ref.shape[1]
    lane = lax.broadcasted_iota(i32, (1, DA_VD), 1)
    lp = lam_ref[...]
    lam = (jnp.exp(jnp.sum(lp[0:1] * lp[1:2], axis=-1, keepdims=True))
           - jnp.exp(jnp.sum(lp[2:3] * lp[3:4], axis=-1, keepdims=True)) + lam_init)

    def stage(t):
        do1, do2, do3 = t < nsub, 1 <= t <= nsub, t >= 2
        s_w, s_r = s_bufs[t % 2], s_bufs[(t - 1) % 2]
        p_w, p_r = p_bufs[(t - 1) % 2], p_bufs[t % 2]
        if do1:
            q = q_ref[0, t * tq:(t + 1) * tq, :]
            qm = [jnp.where(lane < DA_HD, q, jnp.zeros_like(q)), jnp.where(lane >= DA_HD, q, jnp.zeros_like(q))]
            mpart_ref[...] = jnp.full(mpart_ref.shape, -jnp.inf, f32)
        if do2:
            lpart_ref[...] = jnp.zeros(lpart_ref.shape, f32)
        if do3:
            acc_ref[...] = jnp.zeros(acc_ref.shape, f32)

        def chunk(k, v, off, width):
            nb = width // DA_VD
            cols = [slice(off + j * DA_VD, off + (j + 1) * DA_VD) for j in range(nb)]
            if do1:
                for m in range(2):
                    s = _dot_nt(qm[m], k)
                    pm = s[:, 0:DA_VD]
                    for j in range(nb):
                        blk = s[:, j * DA_VD:(j + 1) * DA_VD]
                        s_w[m, :, cols[j]] = blk
                        if j:
                            pm = jnp.maximum(pm, blk)
                    mpart_ref[m] = jnp.maximum(mpart_ref[m], pm)
            if do2:
                for m in range(2):
                    for r0 in range(0, tq, ATT_SLAB):
                        rows = slice(r0, r0 + ATT_SLAB)
                        mr = mrow_ref[m, rows, :]
                        ls = lpart_ref[m, rows, :]
                        for j in range(nb):
                            e = jnp.exp2(s_r[m, rows, cols[j]] - mr)
                            p_w[m, rows, cols[j]] = e
                            ls = ls + e
                        lpart_ref[m, rows, :] = ls
            if do3:
                r = r_ref[...]
                parts = [(p_r[0, :, cols[j]] - r * p_r[1, :, cols[j]]).astype(bf16) for j in range(nb)]
                a = jnp.concatenate(parts, axis=1) if nb > 1 else parts[0]
                acc_ref[...] += _dot(a, v)

        chunk(kc_ref[0], vc_ref[0], 0, n_ctx)
        for c in range(n_lat // ck if n_lat else 0):
            chunk(kl_ref[0, c * ck:(c + 1) * ck, :], vl_ref[0, c * ck:(c + 1) * ck, :], n_ctx + c * ck, ck)

        if do3:
            o = acc_ref[...] * c0_ref[...]
            ms = jnp.mean(o * o, axis=-1, keepdims=True)
            o_ref[0, (t - 2) * tq:(t - 1) * tq, :] = (o * lax.rsqrt(ms + EPS) * g_ref[...] * (1.0 - lam_init)).astype(bf16)
        if do2:
            l0 = jnp.sum(lpart_ref[0], axis=-1, keepdims=True)
            l1 = jnp.sum(lpart_ref[1], axis=-1, keepdims=True)
            c0_ref[...] = jnp.broadcast_to(1.0 / l0, c0_ref.shape)
            r_ref[...] = jnp.broadcast_to(lam * l0 / l1, r_ref.shape)
        if do1:
            for m in range(2):
                mrow_ref[m] = jnp.broadcast_to(jnp.max(mpart_ref[m], axis=-1, keepdims=True), (tq, DA_VD))

    for t in range(nsub + 2):
        stage(t)


def _diff_attn(q, kv_ctx, kv_lat, lam_p, subln_g, lam_init):
    b, lq, _ = q.shape
    tq = min(ATT_TQ, lq)
    nsub = min(ATT_NSUB, lq // tq)
    rows = nsub * tq
    n_ctx = kv_ctx[0].shape[1]
    n_lat = kv_lat[0].shape[1] if kv_lat is not None else 0
    ck = min(ATT_CK, n_lat) if n_lat else 0
    lk = n_ctx + n_lat
    head = lambda bi, h, i: (bi, 0, h)
    in_specs = [pl.BlockSpec((4, DA_HD), lambda bi, h, i: (0, 0)),
                pl.BlockSpec((1, DA_VD), lambda bi, h, i: (0, 0)),
                pl.BlockSpec((1, rows, DA_VD), lambda bi, h, i: (bi, i, h)),
                pl.BlockSpec((1, n_ctx, DA_VD), head), pl.BlockSpec((1, n_ctx, DA_VD), head)]
    args = [lam_p, subln_g.reshape(1, DA_VD), q, kv_ctx[0], kv_ctx[1]]
    if n_lat:
        in_specs += [pl.BlockSpec((1, n_lat, DA_VD), head)] * 2
        args += [kv_lat[0], kv_lat[1]]
    big = pltpu.VMEM((2, tq, lk), f32)
    small = pltpu.VMEM((tq, DA_VD), f32)
    pair = pltpu.VMEM((2, tq, DA_VD), f32)
    return pl.pallas_call(
        functools.partial(_attn_body, nsub, tq, ck, n_lat, lam_init),
        grid=(b, DA_HEADS, lq // rows),
        in_specs=in_specs,
        out_specs=pl.BlockSpec((1, rows, DA_VD), lambda bi, h, i: (bi, i, h)),
        out_shape=jax.ShapeDtypeStruct((b, lq, DA_W), bf16),
        scratch_shapes=[big, big, big, big, pair, pair, pair, small, small, small],
        compiler_params=_params("parallel", "parallel", "parallel"),
        name="diff_attn_lat" if n_lat else "diff_attn_ctx",
    )(*args)


def _outproj_body(n_tok, hy_ref, cf_ref, da_ref, w_ref, x_ref, g2_ref, gn_ref, sh_ref, sc_ref, wr_hi_ref, wr_lo_ref,
                  br_ref, tri_ref, xo_ref, rows_ref, slot_ref, cnt_ref, carry_ref):
    first = (pl.program_id(0) == 0) & (pl.program_id(1) == 0)

    @pl.when(first)
    def _():
        carry_ref[...] = jnp.zeros(carry_ref.shape, f32)

    y = (_dot(hy_ref[0], w_ref[0:HY_W, :]) + _dot(cf_ref[0], w_ref[HY_W:HY_W + CF_W, :])
         + _dot(da_ref[0], w_ref[HY_W + CF_W:MIX_W, :]))
    x = x_ref[0] + g2_ref[0] * y
    xo_ref[0] = x
    ms = jnp.mean(x * x, axis=-1, keepdims=True)
    h = x * lax.rsqrt(ms + EPS) * gn_ref[...]
    h = h * (1.0 + sc_ref[0]) + sh_ref[0]
    rows_ref[0, :, 0:D_MODEL] = h
    tm = h.shape[0]

    h_hi, h_lo = _split_bf16(h)
    logits = _dot_nt(wr_hi_ref[...], h_hi) + _dot_nt(wr_hi_ref[...], h_lo) + _dot_nt(wr_lo_ref[...], h_hi)
    scores = jax.nn.sigmoid(logits)
    sel = scores + br_ref[...]
    srow = [sel[e:e + 1, :] for e in range(N_EXPERTS)]
    crow = [scores[e:e + 1, :] for e in range(N_EXPERTS)]

    best = None
    for g in range(N_GROUPS):
        a, b, c, d = srow[EXPERTS_PER_GROUP * g:EXPERTS_PER_GROUP * (g + 1)]
        hi1, lo1, hi2, lo2 = jnp.maximum(a, b), jnp.minimum(a, b), jnp.maximum(c, d), jnp.minimum(c, d)
        gs = jnp.maximum(hi1, hi2) + jnp.maximum(jnp.minimum(hi1, hi2), jnp.maximum(lo1, lo2))
        if best is None:
            best, gb = gs, jnp.zeros(gs.shape, i32)
        else:
            better = gs > best
            best = jnp.where(better, gs, best)
            gb = jnp.where(better, g, gb)

    def pick(rows, j):
        out = rows[j]
        for g in range(1, N_GROUPS):
            out = jnp.where(gb == g, rows[EXPERTS_PER_GROUP * g + j], out)
        return out

    v = [pick(srow, j) for j in range(EXPERTS_PER_GROUP)]
    sc = [pick(crow, j) for j in range(EXPERTS_PER_GROUP)]

    def argmax_first(vals):
        idx, m = jnp.zeros(vals[0].shape, i32), vals[0]
        for j in range(1, len(vals)):
            better = vals[j] > m
            idx = jnp.where(better, j, idx)
            m = jnp.where(better, vals[j], m)
        return idx

    i1 = argmax_first(v)
    i2 = argmax_first([jnp.where(i1 == j, -jnp.inf, v[j]) for j in range(EXPERTS_PER_GROUP)])
    s1 = functools.reduce(lambda p, q: p + q, [jnp.where(i1 == j, sc[j], 0.0) for j in range(EXPERTS_PER_GROUP)])
    s2 = functools.reduce(lambda p, q: p + q, [jnp.where(i2 == j, sc[j], 0.0) for j in range(EXPERTS_PER_GROUP)])
    inv = 1.0 / (s1 + s2)
    gate = [jnp.where(i1 == j, s1 * inv, 0.0) + jnp.where(i2 == j, s2 * inv, 0.0) for j in range(EXPERTS_PER_GROUP)]

    memb = jnp.concatenate([(gb == g).astype(f32) for g in range(N_GROUPS)]
                           + [jnp.zeros((8 - N_GROUPS, tm), f32)], axis=0)
    before = _dot(memb.astype(bf16), tri_ref[...])
    rank = jnp.sum(memb * (before + carry_ref[:, 0:1]), axis=0, keepdims=True)
    carry_ref[...] = carry_ref[...] + jnp.sum(memb, axis=1, keepdims=True)
    cnt_ref[...] = carry_ref[...]
    slot_ref[0, 0] = (gb.astype(f32) * float(n_tok) + rank).astype(i32)

    tail = jnp.concatenate(gate + [jnp.zeros((MOE_TAIL - EXPERTS_PER_GROUP, tm), f32)], axis=0)
    rows_ref[0, :, D_MODEL:MOE_ROW] = tail.T


def _outproj_router(hy, cf, da, w_out_bf, x, gate2, norm2_g, shift2, scale2, wrt_hi, wrt_lo, b_router_col):
    b, l, d = x.shape
    tm = min(512, l)
    row = lambda bi, i: (bi, i, 0)
    per_b = lambda bi, i: (bi, 0, 0)
    const = lambda bi, i: (0, 0)
    mod = pl.BlockSpec((1, 1, d), per_b)
    tri = (jnp.arange(tm, dtype=i32)[:, None] < jnp.arange(tm, dtype=i32)[None, :]).astype(bf16)
    xo, rows, slot, cnt = pl.pallas_call(
        functools.partial(_outproj_body, b * l),
        grid=(b, l // tm),
        in_specs=[pl.BlockSpec((1, tm, HY_W), row), pl.BlockSpec((1, tm, CF_W), row), pl.BlockSpec((1, tm, DA_W), row),
                  pl.BlockSpec((MIX_W, d), const), pl.BlockSpec((1, tm, d), row), mod,
                  pl.BlockSpec((1, d), const), mod, mod,
                  pl.BlockSpec((N_EXPERTS, d), const), pl.BlockSpec((N_EXPERTS, d), const),
                  pl.BlockSpec((N_EXPERTS, 1), const), pl.BlockSpec((tm, tm), const)],
        out_specs=[pl.BlockSpec((1, tm, d), row), pl.BlockSpec((1, tm, MOE_ROW), row),
                   pl.BlockSpec((1, 1, 1, tm), lambda bi, i: (bi, i, 0, 0)), pl.BlockSpec((8, DA_VD), const)],
        out_shape=[jax.ShapeDtypeStruct((b, l, d), f32), jax.ShapeDtypeStruct((b, l, MOE_ROW), f32),
                   jax.ShapeDtypeStruct((b, l // tm, 1, tm), i32), jax.ShapeDtypeStruct((8, DA_VD), f32)],
        scratch_shapes=[pltpu.VMEM((8, DA_VD), f32)],
        compiler_params=_params("arbitrary", "arbitrary"),
        name="outproj_router",
    )(hy, cf, da, w_out_bf, x, gate2, norm2_g, shift2, scale2, wrt_hi, wrt_lo, b_router_col, tri)
    return xo, rows.reshape(b * l, MOE_ROW), slot.reshape(b * l // tm, 1, tm), cnt


def _row_copy(src_ref, src_row, dst_ref, dst_row, sem):
    return pltpu.make_async_copy(src_ref.at[pl.ds(src_row, 1)], dst_ref.at[pl.ds(dst_row, 1)], sem)


def _dispatch_body(slot_ref, rows_ref, xs_ref, sem):
    tm = rows_ref.shape[0]

    def issue(r, carry):
        _row_copy(rows_ref, r, xs_ref, slot_ref[0, 0, r], sem).start()
        return carry

    lax.fori_loop(0, tm, issue, 0, unroll=8)
    pltpu.make_async_copy(rows_ref, xs_ref.at[pl.ds(0, tm)], sem).wait()


def _dispatch(rows, slot, n_rows):
    n_tok = rows.shape[0]
    tm = slot.shape[2]
    return pl.pallas_call(
        _dispatch_body,
        grid=(n_tok // tm,),
        in_specs=[pl.BlockSpec((1, 1, tm), lambda i: (i, 0, 0), memory_space=pltpu.SMEM),
                  pl.BlockSpec((tm, MOE_ROW), lambda i: (i, 0))],
        out_specs=pl.BlockSpec(memory_space=pl.ANY),
        out_shape=jax.ShapeDtypeStruct((n_rows, MOE_ROW), f32),
        scratch_shapes=[pltpu.SemaphoreType.DMA],
        compiler_params=_params("arbitrary"),
        name="moe_dispatch",
    )(slot, rows)


def _ffn_body(blk_ref, oblk_ref, grp_ref, nrow_ref, xs_ref, w1_ref, w3_ref, w2_ref, ys_ref, acc_ref):
    i = pl.program_id(0)
    j = pl.program_id(1)
    nrow = nrow_ref[i]

    @pl.when(nrow > 0)
    def _():
        tmf = xs_ref.shape[0]
        valid = lax.broadcasted_iota(i32, (tmf, 1), 0) < nrow
        h = jnp.where(valid, xs_ref[:, 0:D_MODEL], 0.0).astype(bf16)
        tail = xs_ref[:, D_MODEL:MOE_ROW]
        lane = lax.broadcasted_iota(i32, tail.shape, 1)
        gate = jnp.sum(jnp.where(valid & (lane == j), tail, 0.0), axis=-1, keepdims=True)
        a = _dot(h, w1_ref[0])
        a = a * jax.nn.sigmoid(a) * _dot(h, w3_ref[0])
        y = gate * _dot(a.astype(bf16), w2_ref[0])

        @pl.when(j == 0)
        def _():
            acc_ref[...] = y

        @pl.when(j > 0)
        def _():
            acc_ref[...] += y

        @pl.when(j == EXPERTS_PER_GROUP - 1)
        def _():
            ys_ref[...] = acc_ref[...]


def _ffn(xs, blk, oblk, grp, nrow, w1_bf, w3_bf, w2_bf, tmf):
    n_rows = xs.shape[0]
    d = D_MODEL
    wmap = lambda i, j, blk, oblk, grp, nrow: (grp[i] * EXPERTS_PER_GROUP + j, 0, 0)
    grid_spec = pltpu.PrefetchScalarGridSpec(
        num_scalar_prefetch=4,
        grid=(blk.shape[0], EXPERTS_PER_GROUP),
        in_specs=[pl.BlockSpec((tmf, MOE_ROW), lambda i, j, blk, oblk, grp, nrow: (blk[i], 0)),
                  pl.BlockSpec((1, d, D_EXPERT), wmap), pl.BlockSpec((1, d, D_EXPERT), wmap),
                  pl.BlockSpec((1, D_EXPERT, d), wmap)],
        out_specs=pl.BlockSpec((tmf, d), lambda i, j, blk, oblk, grp, nrow: (oblk[i], 0)),
        scratch_shapes=[pltpu.VMEM((tmf, d), f32)])
    return pl.pallas_call(
        _ffn_body,
        grid_spec=grid_spec,
        out_shape=jax.ShapeDtypeStruct((n_rows + tmf, d), f32),
        compiler_params=_params("arbitrary", "arbitrary"),
        name="moe_ffn",
    )(blk, oblk, grp, nrow, xs, w1_bf, w3_bf, w2_bf)


def _combine_body(slot_ref, ys_ref, x_ref, g5_ref, o_ref, buf_ref, sem):
    tm = buf_ref.shape[0]

    def issue(r, carry):
        _row_copy(ys_ref, slot_ref[0, 0, r], buf_ref, r, sem).start()
        return carry

    lax.fori_loop(0, tm, issue, 0, unroll=8)
    pltpu.make_async_copy(ys_ref.at[pl.ds(0, tm)], buf_ref, sem).wait()
    o_ref[0] = x_ref[0] + g5_ref[0] * buf_ref[...]


def _combine(ys, slot, x, gate5):
    b, l, d = x.shape
    tm = slot.shape[2]
    nt = l // tm
    return pl.pallas_call(
        _combine_body,
        grid=(b, nt),
        in_specs=[pl.BlockSpec((1, 1, tm), lambda bi, i: (bi * nt + i, 0, 0), memory_space=pltpu.SMEM),
                  pl.BlockSpec(memory_space=pl.ANY),
                  pl.BlockSpec((1, tm, d), lambda bi, i: (bi, i, 0)),
                  pl.BlockSpec((1, 1, d), lambda bi, i: (bi, 0, 0))],
        out_specs=pl.BlockSpec((1, tm, d), lambda bi, i: (bi, i, 0)),
        out_shape=jax.ShapeDtypeStruct((b, l, d), f32),
        scratch_shapes=[pltpu.VMEM((tm, d), f32), pltpu.SemaphoreType.DMA],
        compiler_params=_params("arbitrary", "arbitrary"),
        name="moe_combine",
    )(slot, ys, x, gate5)


def _tile_tables(cnt, n_tok, tmf, n_tiles):
    counts = cnt[:N_GROUPS, 0].astype(i32)
    tiles = (counts + tmf - 1) // tmf
    ends = jnp.cumsum(tiles)
    i = jnp.arange(n_tiles, dtype=i32)
    grp = jnp.minimum(jnp.sum((i[:, None] >= ends[None, :]).astype(i32), axis=1), N_GROUPS - 1)
    k = i - (ends - tiles)[grp]
    used = i < ends[-1]
    blk = jnp.where(used, grp * (n_tok // tmf) + k, 0)
    oblk = jnp.where(used, blk, N_GROUPS * n_tok // tmf)
    nrow = jnp.where(used, jnp.clip(counts[grp] - k * tmf, 0, tmf), 0)
    return blk.astype(i32), oblk.astype(i32), grp.astype(i32), nrow.astype(i32)


def _moe(rows, slot, cnt, x, gate5, w1_bf, w3_bf, w2_bf):
    n_tok = rows.shape[0]
    tmf = min(512, n_tok)
    n_tiles = n_tok // tmf + N_GROUPS
    xs = _dispatch(rows, slot, N_GROUPS * n_tok)
    blk, oblk, grp, nrow = _tile_tables(cnt, n_tok, tmf, n_tiles)
    ys = _ffn(xs, blk, oblk, grp, nrow, w1_bf, w3_bf, w2_bf, tmf)
    return _combine(ys, slot, x, gate5)


def _rope_tables(n_lat):
    rows = n_lat // GRID_W
    row = jnp.repeat(jnp.arange(rows, dtype=f32), GRID_W)
    col = jnp.tile(jnp.arange(GRID_W, dtype=f32), rows)
    inv = ROPE_THETA ** (-jnp.arange(ROPE_AXIS_FREQS, dtype=f32) / ROPE_AXIS_FREQS)
    ang = jnp.concatenate([row[:, None] * inv, col[:, None] * inv], axis=-1)
    cos, sin = jnp.cos(ang), jnp.sin(ang)
    reps = DA_QK // DA_HD
    return jnp.tile(jnp.concatenate([cos, cos], -1), (1, reps)), jnp.tile(jnp.concatenate([-sin, sin], -1), (1, reps))


def _dft_tables(l):
    n = 2 * l
    f = jnp.arange(l, dtype=i32)[:, None]
    t = jnp.arange(l, dtype=i32)[None, :]
    ang = ((f * t) % n).astype(f32) * (2.0 * math.pi / n)
    cm = jnp.cos(ang)
    sm = jnp.where(f == 0, jnp.where(t % 2 == 0, 1.0, -1.0), jnp.sin(ang))
    return cm.astype(bf16), sm.astype(bf16), sm.T.astype(bf16)


def _hy_features(l):
    t_idx = jnp.arange(l, dtype=f32)[:, None]
    t01 = t_idx / max(l - 1, 1)
    bands = jnp.linspace(1e-4, HY_BANDS - 1, HY_BANDS, dtype=f32)
    ang = 2.0 * math.pi * bands * t_idx / l
    feats = jnp.concatenate([t01, jnp.cos(ang), -jnp.sin(ang)], axis=-1)
    return jnp.pad(feats, ((0, 0), (0, HY_EMB_PAD - HY_EMB)))


def _hyena(u_hy, dft, feats, deltas, conv_w, conv_b, w1p, b1, w2, b2, w3, freq, bias):
    cm, sm, smt = dft
    x0, z = _hyprep(u_hy, conv_w, conv_b)
    hs, hd = _hy_filter(feats, w1p, b1, w2, b2, w3, freq, deltas)
    a, bco, dco = _hy_spectrum(cm, sm, hs, hd)
    return _fftconv(z, x0, cm, sm, smt, a, bco, dco, bias)


def kernel(x, c, ctx, c_ctx, w_ada, b_ada, norm1_g, norm2_g, w_in, hy_conv_w, hy_conv_b, hy_w1, hy_b1, hy_w2, hy_b2,
           hy_w3, hy_freq, hy_bias, cf_dw_w, cf_dw_b, cf_ln_g, cf_ln_b, da_qn_g, da_kn_g, da_lam, da_subln_g, w_out,
           w_router, b_router, moe_w1, moe_w3, moe_w2):
    depth = w_ada.shape[0]
    bsz, n_lat, d = x.shape
    n_ctx = ctx.shape[1]

    rope = _rope_tables(n_lat)
    dft_lat, dft_ctx = _dft_tables(n_lat), _dft_tables(n_ctx)
    feats_lat, feats_ctx = _hy_features(n_lat), _hy_features(n_ctx)
    deltas = jnp.abs(jnp.linspace(HY_MIN_DECAY, HY_MAX_DECAY, HY_W, dtype=f32)).reshape(1, HY_W)
    seg = jnp.arange(DA_QK, dtype=i32) // DA_HD
    bd = (seg[:, None] == seg[None, :]).astype(bf16)

    rows = jnp.concatenate([c, c_ctx[None], jnp.zeros((7, d), f32)], axis=0)
    mod_all = _ada(rows, w_ada, b_ada)

    wr_hi = w_router.T.astype(bf16)
    wr_lo = (w_router.T - wr_hi.astype(f32)).astype(bf16)
    br = b_router.reshape(N_EXPERTS, 1)
    hy_w1p = jnp.pad(hy_w1, ((0, 0), (0, HY_EMB_PAD - HY_EMB), (0, 0)))

    x_lat, x_ctx = x, ctx
    for l in range(depth):
        last = l == depth - 1
        lam_init = 0.8 - 0.6 * math.exp(-0.3 * l)
        mod = mod_all[l, :bsz].reshape(bsz, 6, 1, d)
        mod_c = jnp.broadcast_to(mod_all[l, bsz].reshape(1, 6, 1, d), (bsz, 6, 1, d))
        w_in_bf = w_in[l].astype(bf16)
        w_out_bf = w_out[l].astype(bf16)
        w1_bf, w3_bf, w2_bf = moe_w1[l].astype(bf16), moe_w3[l].astype(bf16), moe_w2[l].astype(bf16)
        g1 = norm1_g[l].reshape(1, d)
        g2 = norm2_g[l].reshape(1, d)
        qg = jnp.tile(da_qn_g[l], DA_QK // DA_HD).reshape(1, DA_QK)
        kg = jnp.tile(da_kn_g[l], DA_QK // DA_HD).reshape(1, DA_QK)
        hy_p = (hy_conv_w[l], hy_conv_b[l], hy_w1p[l], hy_b1[l], hy_w2[l], hy_b2[l], hy_w3[l], hy_freq[l], hy_bias[l])
        cf_p = (cf_dw_w[l], cf_dw_b[l], cf_ln_g[l], cf_ln_b[l])

        uhy, ucf, q_lat, k_lat, v_lat = _inproj(x_lat, mod[:, 0], mod[:, 1], g1, w_in_bf, qg, kg, bd, rope)
        uhy_c, ucf_c, q_ctx, k_ctx, v_ctx = _inproj(x_ctx, mod_c[:, 0], mod_c[:, 1], g1, w_in_bf, qg, kg, bd, None)
        hy_lat = _hyena(uhy, dft_lat, feats_lat, deltas, *hy_p)
        cf_lat = _conformer(ucf, *cf_p)
        da_lat = _diff_attn(q_lat, (k_ctx, v_ctx), (k_lat, v_lat), da_lam[l], da_subln_g[l], lam_init)
        x_lat, rows, slot, cnt = _outproj_router(hy_lat, cf_lat, da_lat, w_out_bf, x_lat, mod[:, 2], g2, mod[:, 3],
                                                 mod[:, 4], wr_hi, wr_lo, br)
        x_lat = _moe(rows, slot, cnt, x_lat, mod[:, 5], w1_bf, w3_bf, w2_bf)
        if not last:
            hy_c = _hyena(uhy_c, dft_ctx, feats_ctx, deltas, *hy_p)
            cf_c = _conformer(ucf_c, *cf_p)
            da_c = _diff_attn(q_ctx, (k_ctx, v_ctx), None, da_lam[l], da_subln_g[l], lam_init)
            x_ctx, rows_c, slot_c, cnt_c = _outproj_router(hy_c, cf_c, da_c, w_out_bf, x_ctx, mod_c[:, 2], g2,
                                                           mod_c[:, 3], mod_c[:, 4], wr_hi, wr_lo, br)
            x_ctx = _moe(rows_c, slot_c, cnt_c, x_ctx, mod_c[:, 5], w1_bf, w3_bf, w2_bf)
    return x_lat
```

```python
import functools
import math

import jax
import jax.numpy as jnp
from jax import lax
from jax.experimental import pallas as pl
from jax.experimental.pallas import tpu as pltpu

f32 = jnp.float32
bf16 = jnp.bfloat16
i32 = jnp.int32

D_MODEL = 1024
GRID_W = 64
HY_W = 256
CF_W = 256
DA_HEADS = 4
DA_HD = 64
DA_VD = 2 * DA_HD
DA_W = DA_HEADS * DA_VD
MIX_W = HY_W + CF_W + DA_W
HY_IN = 3 * HY_W
CF_IN = 2 * CF_W
DA_QK = DA_HEADS * 2 * DA_HD
OFF_CF = HY_IN
OFF_Q = OFF_CF + CF_IN
OFF_K = OFF_Q + DA_QK
OFF_V = OFF_K + DA_QK
IN_W = OFF_V + DA_W
CF_CONV = 31
HY_BANDS = 16
HY_EMB = 1 + 2 * HY_BANDS
HY_EMB_PAD = 64
HY_FF = 64
HY_MIN_DECAY = math.log(1e-2) / 1.5
HY_MAX_DECAY = math.log(1e-2) / 0.3
ROPE_THETA = 10000.0
ROPE_AXIS_FREQS = DA_HD // 4
N_EXPERTS = 16
N_GROUPS = 4
EXPERTS_PER_GROUP = N_EXPERTS // N_GROUPS
D_EXPERT = 512
EPS = 1e-6

VMEM_LIMIT_BYTES = 56 * 1024 * 1024
CF_HALO = 16
BF16_ROWS = 16
ATT_TQ = 256
ATT_SLAB = 64
ATT_NSUB = 4
ATT_CK = 512
LOG2E = 1.4426950408889634
MOE_TAIL = 128
MOE_ROW = D_MODEL + MOE_TAIL
MOE_TMF = 256
TOP_K = 2
PAIR_LO = (0, 0, 0, 1, 1, 2)
PAIR_HI = (1, 2, 3, 2, 3, 3)
N_PAIRS = len(PAIR_LO)
N_BUCKETS = N_GROUPS * N_PAIRS
BUCKET_ROWS = 32


def _params(*sem):
    return pltpu.CompilerParams(dimension_semantics=sem, vmem_limit_bytes=VMEM_LIMIT_BYTES)


def _dot(a, b):
    return jnp.dot(a, b, preferred_element_type=f32)


def _dot_nt(a, b):
    return lax.dot_general(a, b, (((1,), (1,)), ((), ())), preferred_element_type=f32)


def _split_bf16(v):
    hi = v.astype(bf16)
    lo = (v - hi.astype(f32)).astype(bf16)
    return hi, lo


def _ada_body(c_ref, w_ref, b_ref, o_ref):
    c = c_ref[...]
    s = c * jax.nn.sigmoid(c)
    o_ref[0] = _dot(s, w_ref[0]) + b_ref[0]


def _ada(cc, w_ada, b_ada):
    depth, d, n = w_ada.shape
    r = cc.shape[0]
    tn = 1536
    return pl.pallas_call(
        _ada_body,
        grid=(depth, n // tn),
        in_specs=[pl.BlockSpec((r, d), lambda l, j: (0, 0)),
                  pl.BlockSpec((1, d, tn), lambda l, j: (l, 0, j)),
                  pl.BlockSpec((1, 1, tn), lambda l, j: (l, 0, j))],
        out_specs=pl.BlockSpec((1, r, tn), lambda l, j: (l, 0, j)),
        out_shape=jax.ShapeDtypeStruct((depth, r, n), f32),
        compiler_params=_params("parallel", "parallel"),
        name="ada_mod",
    )(cc, w_ada, b_ada.reshape(depth, 1, n))


def _inproj_body(rope, x_ref, sh_ref, sc_ref, g_ref, w_ref, qg_ref, kg_ref, bd_ref, *rest):
    if rope:
        cos_ref, sin_ref, hy_ref, cf_ref, q_ref, k_ref, v_ref = rest
    else:
        hy_ref, cf_ref, q_ref, k_ref, v_ref = rest
    x = x_ref[0]
    ms = jnp.mean(x * x, axis=-1, keepdims=True)
    h = x * lax.rsqrt(ms + EPS) * g_ref[...]
    h = h * (1.0 + sc_ref[0]) + sh_ref[0]
    hb = h.astype(bf16)

    def proj(lo, hi):
        return _dot(hb, w_ref[:, lo:hi])

    hy_ref[0] = proj(0, OFF_CF).astype(bf16)
    cf_ref[0] = proj(OFF_CF, OFF_Q).astype(bf16)
    v_ref[0] = proj(OFF_V, IN_W).astype(bf16)

    lane = lax.broadcasted_iota(i32, (1, DA_QK), 1)
    first_half = (lane % DA_HD) < (DA_HD // 2)

    def qk_heads(lo, hi, gain_ref, out_ref, scale):
        t = proj(lo, hi)
        ss = _dot((t * t).astype(bf16), bd_ref[...])
        tn = t * lax.rsqrt(ss * (1.0 / DA_HD) + EPS) * gain_ref[...]
        if rope:
            half = DA_HD // 2
            partner = jnp.where(first_half, pltpu.roll(tn, DA_QK - half, 1), pltpu.roll(tn, half, 1))
            tn = tn * cos_ref[...] + partner * sin_ref[...]
        out_ref[0] = (tn * scale).astype(bf16)

    qk_heads(OFF_Q, OFF_K, qg_ref, q_ref, LOG2E * DA_HD ** -0.5)
    qk_heads(OFF_K, OFF_V, kg_ref, k_ref, 1.0)


def _inproj(x, shift, scale, g, w_bf, qg, kg, bd, rope_tabs):
    b, l, d = x.shape
    tm = min(512, l)
    rope = rope_tabs is not None
    row = lambda bi, i: (bi, i, 0)
    per_b = lambda bi, i: (bi, 0, 0)
    const = lambda bi, i: (0, 0)
    in_specs = [pl.BlockSpec((1, tm, d), row),
                pl.BlockSpec((1, 1, d), per_b),
                pl.BlockSpec((1, 1, d), per_b),
                pl.BlockSpec((1, d), const),
                pl.BlockSpec((d, IN_W), const),
                pl.BlockSpec((1, DA_QK), const),
                pl.BlockSpec((1, DA_QK), const),
                pl.BlockSpec((DA_QK, DA_QK), const)]
    args = [x, shift, scale, g, w_bf, qg, kg, bd]
    if rope:
        in_specs += [pl.BlockSpec((tm, DA_QK), lambda bi, i: (i, 0))] * 2
        args += list(rope_tabs)
    widths = (HY_IN, CF_IN, DA_QK, DA_QK, DA_W)
    return pl.pallas_call(
        functools.partial(_inproj_body, rope),
        grid=(b, l // tm),
        in_specs=in_specs,
        out_specs=[pl.BlockSpec((1, tm, w), row) for w in widths],
        out_shape=[jax.ShapeDtypeStruct((b, l, w), bf16) for w in widths],
        compiler_params=_params("parallel", "parallel"),
        name="inproj_rope" if rope else "inproj_ctx",
    )(*args)


def _hyprep_body(u_ref, w_ref, b_ref, x0_ref, z_ref):
    l = u_ref.shape[1]
    t = min(128, l)
    n = l // t
    row = lax.broadcasted_iota(i32, (t, 1), 0)

    def chunk(i, carry):
        t0 = pl.multiple_of(i * t, t)
        tp = pl.multiple_of(jnp.maximum(t0 - BF16_ROWS, 0), BF16_ROWS)
        tx = pl.multiple_of(jnp.minimum(t0 + t, l - BF16_ROWS), BF16_ROWS)
        outs = []
        for gi in range(3):
            ls = slice(gi * HY_W, (gi + 1) * HY_W)
            a = u_ref[0, pl.ds(t0, t), ls].astype(f32)
            prev = u_ref[0, pl.ds(tp, BF16_ROWS), ls].astype(f32)[BF16_ROWS - 1:BF16_ROWS]
            prev = jnp.where(i > 0, prev, 0.0)
            nxt = u_ref[0, pl.ds(tx, BF16_ROWS), ls].astype(f32)[0:1]
            nxt = jnp.where(i < n - 1, nxt, 0.0)
            um = jnp.where(row == 0, prev, pltpu.roll(a, 1, 0))
            up = jnp.where(row == t - 1, nxt, pltpu.roll(a, t - 1, 0))
            w = w_ref[:, ls]
            outs.append(w[0:1] * um + w[1:2] * a + w[2:3] * up + b_ref[:, ls])
        x0, x1, v = outs
        x0_ref[0, pl.ds(t0, t), :] = x0.astype(bf16)
        z_ref[0, pl.ds(t0, t), :] = (x1 * v).astype(bf16)
        return carry

    lax.fori_loop(0, n, chunk, 0)


def _hyprep(u_hy, conv_w, conv_b):
    b, l, _ = u_hy.shape
    per_b = lambda bi: (bi, 0, 0)
    const = lambda bi: (0, 0)
    return pl.pallas_call(
        _hyprep_body,
        grid=(b,),
        in_specs=[pl.BlockSpec((1, l, HY_IN), per_b),
                  pl.BlockSpec((3, HY_IN), const),
                  pl.BlockSpec((1, HY_IN), const)],
        out_specs=[pl.BlockSpec((1, l, HY_W), per_b)] * 2,
        out_shape=[jax.ShapeDtypeStruct((b, l, HY_W), bf16)] * 2,
        compiler_params=_params("parallel"),
        name="hyena_prep",
    )(u_hy, conv_w, conv_b.reshape(1, HY_IN))


def _filter_body(feat_ref, w1_ref, b1_ref, w2_ref, b2_ref, w3_ref, fr_ref, dl_ref, hs_ref, hd_ref):
    tl = feat_ref.shape[0]
    feats = feat_ref[...]
    fr = fr_ref[...]
    hid = jnp.sin(fr * (_dot(feats, w1_ref[...]) + b1_ref[...]))
    hid = jnp.sin(fr * (_dot(hid, w2_ref[...]) + b2_ref[...]))
    h = _dot(hid, w3_ref[...])
    win = jnp.exp(-feats[:, 0:1] * dl_ref[...])
    fwd = h[:, :HY_W] * win
    bwd = h[:, HY_W:] * win
    row = pl.program_id(0) * tl + lax.broadcasted_iota(i32, (tl, 1), 0)
    bwd = jnp.where(row == 0, 0.0, bwd)
    hs_ref[...] = fwd + bwd
    hd_ref[...] = bwd - fwd


def _hy_filter(feats, w1p, b1, w2, b2, w3, freq, deltas):
    l = feats.shape[0]
    tl = min(512, l)
    const = lambda i: (0, 0)
    return pl.pallas_call(
        _filter_body,
        grid=(l // tl,),
        in_specs=[pl.BlockSpec((tl, HY_EMB_PAD), lambda i: (i, 0)),
                  pl.BlockSpec((HY_EMB_PAD, HY_FF), const),
                  pl.BlockSpec((1, HY_FF), const),
                  pl.BlockSpec((HY_FF, HY_FF), const),
                  pl.BlockSpec((1, HY_FF), const),
                  pl.BlockSpec((HY_FF, 2 * HY_W), const),
                  pl.BlockSpec((1, HY_FF), const),
                  pl.BlockSpec((1, HY_W), const)],
        out_specs=[pl.BlockSpec((tl, HY_W), lambda i: (i, 0))] * 2,
        out_shape=[jax.ShapeDtypeStruct((l, HY_W), f32)] * 2,
        compiler_params=_params("parallel"),
        name="hyena_filter",
    )(feats, w1p, b1.reshape(1, HY_FF), w2, b2.reshape(1, HY_FF), w3, freq.reshape(1, HY_FF), deltas)


def _kf_body(inv_n, c_ref, s_ref, hs_ref, hd_ref, a_ref, b_ref, d_ref):
    tf = c_ref.shape[0]
    l = hs_ref.shape[0]
    hs = hs_ref[...]
    hs_hi, hs_lo = _split_bf16(hs)
    hd_hi, hd_lo = _split_bf16(hd_ref[...])
    kre = _dot(c_ref[...], hs_hi) + _dot(c_ref[...], hs_lo)
    kim = _dot(s_ref[...], hd_hi) + _dot(s_ref[...], hd_lo)
    tpos = lax.broadcasted_iota(i32, (l, 1), 0)
    knyq = jnp.sum(jnp.where(tpos % 2 == 0, hs, -hs), axis=0, keepdims=True)
    row = pl.program_id(0) * tf + lax.broadcasted_iota(i32, (tf, 1), 0)
    dc = row == 0
    a_ref[...] = jnp.where(dc, kre * inv_n, kre * (2.0 * inv_n))
    b_ref[...] = jnp.where(dc, 0.0, kim * (2.0 * inv_n))
    d_ref[...] = jnp.where(dc, knyq * inv_n, kre * (2.0 * inv_n))


def _hy_spectrum(cm, sm, hs, hd):
    l = hs.shape[0]
    tf = min(512, l)
    full = lambda i: (0, 0)
    rows = lambda i: (i, 0)
    return pl.pallas_call(
        functools.partial(_kf_body, 1.0 / (2 * l)),
        grid=(l // tf,),
        in_specs=[pl.BlockSpec((tf, l), rows), pl.BlockSpec((tf, l), rows),
                  pl.BlockSpec((l, HY_W), full), pl.BlockSpec((l, HY_W), full)],
        out_specs=[pl.BlockSpec((tf, HY_W), rows)] * 3,
        out_shape=[jax.ShapeDtypeStruct((l, HY_W), f32)] * 3,
        compiler_params=_params("parallel"),
        name="hyena_spectrum",
    )(cm, sm, hs, hd)


def _fftconv_body(z_ref, x0_ref, cr_ref, sr_ref, cc_ref, sc_ref, a_ref, b_ref, d_ref, bias_ref, o_ref, acc_ref):
    j = pl.program_id(1)
    bco = b_ref[...]
    for bi in range(z_ref.shape[0]):
        z = z_ref[bi]
        zre = _dot(cr_ref[...], z)
        zs = _dot(sr_ref[...], z)
        yre = (a_ref[...] * zre + bco * zs).astype(bf16)
        yim = (d_ref[...] * zs - bco * zre).astype(bf16)
        part = _dot(cc_ref[...], yre) + _dot(sc_ref[...], yim)

        @pl.when(j == 0)
        def _():
            acc_ref[bi] = part

        @pl.when(j > 0)
        def _():
            acc_ref[bi] += part

        @pl.when(j == pl.num_programs(1) - 1)
        def _():
            y = acc_ref[bi] + bias_ref[...] * z.astype(f32)
            o_ref[bi] = (x0_ref[bi].astype(f32) * y).astype(bf16)


def _fftconv(z, x0, cm, sm, smt, a, bco, dco, bias):
    b, l, _ = z.shape
    tf = min(256, l)
    nb = 2 if b % 2 == 0 else 1
    per_b = lambda bi, j: (bi, 0, 0)
    frow = lambda bi, j: (j, 0)
    fcol = lambda bi, j: (0, j)
    return pl.pallas_call(
        _fftconv_body,
        grid=(b // nb, l // tf),
        in_specs=[pl.BlockSpec((nb, l, HY_W), per_b), pl.BlockSpec((nb, l, HY_W), per_b),
                  pl.BlockSpec((tf, l), frow), pl.BlockSpec((tf, l), frow),
                  pl.BlockSpec((l, tf), fcol), pl.BlockSpec((l, tf), fcol),
                  pl.BlockSpec((tf, HY_W), frow), pl.BlockSpec((tf, HY_W), frow), pl.BlockSpec((tf, HY_W), frow),
                  pl.BlockSpec((1, HY_W), lambda bi, j: (0, 0))],
        out_specs=pl.BlockSpec((nb, l, HY_W), per_b),
        out_shape=jax.ShapeDtypeStruct((b, l, HY_W), bf16),
        scratch_shapes=[pltpu.VMEM((nb, l, HY_W), f32)],
        compiler_params=_params("parallel", "arbitrary"),
        name="hyena_fftconv",
    )(z, x0, cm, sm, cm, smt, a, bco, dco, bias.reshape(1, HY_W))


def _conformer_body(u_ref, up_ref, un_ref, w_ref, b_ref, g_ref, beta_ref, o_ref, ext_ref):
    i = pl.program_id(1)
    tc = u_ref.shape[1]
    t = min(128, tc)

    def glu(ref, rows):
        a = ref[0, rows, 0:CF_W].astype(f32)
        g = ref[0, rows, CF_W:CF_IN].astype(f32)
        return a * jax.nn.sigmoid(g)

    ext_ref[0:CF_HALO, :] = jnp.where(i > 0, glu(up_ref, slice(None)), 0.0)
    ext_ref[CF_HALO + tc:CF_HALO + tc + CF_HALO, :] = jnp.where(i < pl.num_programs(1) - 1, glu(un_ref, slice(None)), 0.0)
    for t0 in range(0, tc, t):
        ext_ref[CF_HALO + t0:CF_HALO + t0 + t, :] = glu(u_ref, slice(t0, t0 + t))

    for t0 in range(0, tc, t):
        acc = jnp.zeros((t, CF_W), f32) + b_ref[...]
        for j in range(CF_CONV):
            lo = t0 + CF_HALO - CF_CONV // 2 + j
            acc = acc + w_ref[j:j + 1, :] * ext_ref[lo:lo + t, :]
        mu = jnp.mean(acc, axis=-1, keepdims=True)
        xc = acc - mu
        var = jnp.mean(xc * xc, axis=-1, keepdims=True)
        y = xc * lax.rsqrt(var + EPS) * g_ref[...] + beta_ref[...]
        o_ref[0, t0:t0 + t, :] = (y * jax.nn.sigmoid(y)).astype(bf16)


def _conformer(u_cf, dw_w, dw_b, ln_g, ln_b):
    b, l, _ = u_cf.shape
    tc = min(512, l)
    hb = tc // CF_HALO
    nh = l // CF_HALO
    const = lambda bi, i: (0, 0)
    vec = pl.BlockSpec((1, CF_W), const)
    return pl.pallas_call(
        _conformer_body,
        grid=(b, l // tc),
        in_specs=[pl.BlockSpec((1, tc, CF_IN), lambda bi, i: (bi, i, 0)),
                  pl.BlockSpec((1, CF_HALO, CF_IN), lambda bi, i: (bi, jnp.maximum(i * hb - 1, 0), 0)),
                  pl.BlockSpec((1, CF_HALO, CF_IN), lambda bi, i: (bi, jnp.minimum((i + 1) * hb, nh - 1), 0)),
                  pl.BlockSpec((CF_CONV, CF_W), const), vec, vec, vec],
        out_specs=pl.BlockSpec((1, tc, CF_W), lambda bi, i: (bi, i, 0)),
        out_shape=jax.ShapeDtypeStruct((b, l, CF_W), bf16),
        scratch_shapes=[pltpu.VMEM((tc + 2 * CF_HALO, CF_W), f32)],
        compiler_params=_params("parallel", "parallel"),
        name="conformer_conv",
    )(u_cf, u_cf, u_cf, dw_w, dw_b.reshape(1, CF_W), ln_g.reshape(1, CF_W), ln_b.reshape(1, CF_W))


def _attn_body(nsub, tq, ck, n_lat, lam_init, lam_ref, g_ref, q_ref, kc_ref, vc_ref, *rest):
    if n_lat:
        kl_ref, vl_ref = rest[:2]
        rest = rest[2:]
    o_ref, s_a, s_b, p_a, p_b, mpart_ref, mrow_ref, lpart_ref, c0_ref, r_ref, acc_ref = rest
    s_bufs, p_bufs = (s_a, s_b), (p_a, p_b)
    n_ctx = kc_ref.shape[1]
    lane = lax.broadcasted_iota(i32, (1, DA_VD), 1)
    lp = lam_ref[...]
    lam = (jnp.exp(jnp.sum(lp[0:1] * lp[1:2], axis=-1, keepdims=True))
           - jnp.exp(jnp.sum(lp[2:3] * lp[3:4], axis=-1, keepdims=True)) + lam_init)

    def stage(t):
        do1, do2, do3 = t < nsub, 1 <= t <= nsub, t >= 2
        s_w, s_r = s_bufs[t % 2], s_bufs[(t - 1) % 2]
        p_w, p_r = p_bufs[(t - 1) % 2], p_bufs[t % 2]
        if do1:
            q = q_ref[0, t * tq:(t + 1) * tq, :]
            qm = [jnp.where(lane < DA_HD, q, jnp.zeros_like(q)), jnp.where(lane >= DA_HD, q, jnp.zeros_like(q))]
            mpart_ref[...] = jnp.full(mpart_ref.shape, -jnp.inf, f32)
        if do2:
            lpart_ref[...] = jnp.zeros(lpart_ref.shape, f32)
        if do3:
            acc_ref[...] = jnp.zeros(acc_ref.shape, f32)

        def chunk(k, v, off, width):
            nb = width // DA_VD
            cols = [slice(off + j * DA_VD, off + (j + 1) * DA_VD) for j in range(nb)]
            if do1:
                for m in range(2):
                    s = _dot_nt(qm[m], k)
                    pm = s[:, 0:DA_VD]
                    for j in range(nb):
                        blk = s[:, j * DA_VD:(j + 1) * DA_VD]
                        s_w[m, :, cols[j]] = blk
                        if j:
                            pm = jnp.maximum(pm, blk)
                    mpart_ref[m] = jnp.maximum(mpart_ref[m], pm)
            if do2:
                for m in range(2):
                    for r0 in range(0, tq, ATT_SLAB):
                        rows = slice(r0, r0 + ATT_SLAB)
                        mr = mrow_ref[m, rows, :]
                        ls = lpart_ref[m, rows, :]
                        for j in range(nb):
                            e = jnp.exp2(s_r[m, rows, cols[j]] - mr)
                            p_w[m, rows, cols[j]] = e
                            ls = ls + e
                        lpart_ref[m, rows, :] = ls
            if do3:
                r = r_ref[...]
                parts = [(p_r[0, :, cols[j]] - r * p_r[1, :, cols[j]]).astype(bf16) for j in range(nb)]
                a = jnp.concatenate(parts, axis=1) if nb > 1 else parts[0]
                acc_ref[...] += _dot(a, v)

        chunk(kc_ref[0], vc_ref[0], 0, n_ctx)
        for c in range(n_lat // ck if n_lat else 0):
            chunk(kl_ref[0, c * ck:(c + 1) * ck, :], vl_ref[0, c * ck:(c + 1) * ck, :], n_ctx + c * ck, ck)

        if do3:
            o = acc_ref[...] * c0_ref[...]
            ms = jnp.mean(o * o, axis=-1, keepdims=True)
            o_ref[0, (t - 2) * tq:(t - 1) * tq, :] = (o * lax.rsqrt(ms + EPS) * g_ref[...] * (1.0 - lam_init)).astype(bf16)
        if do2:
            l0 = jnp.sum(lpart_ref[0], axis=-1, keepdims=True)
            l1 = jnp.sum(lpart_ref[1], axis=-1, keepdims=True)
            c0_ref[...] = jnp.broadcast_to(1.0 / l0, c0_ref.shape)
            r_ref[...] = jnp.broadcast_to(lam * l0 / l1, r_ref.shape)
        if do1:
            for m in range(2):
                mrow_ref[m] = jnp.broadcast_to(jnp.max(mpart_ref[m], axis=-1, keepdims=True), (tq, DA_VD))

    for t in range(nsub + 2):
        stage(t)


def _diff_attn(q, kv_ctx, kv_lat, lam_p, subln_g, lam_init):
    b, lq, _ = q.shape
    tq = min(ATT_TQ, lq)
    nsub = min(ATT_NSUB, lq // tq)
    rows = nsub * tq
    n_ctx = kv_ctx[0].shape[1]
    n_lat = kv_lat[0].shape[1] if kv_lat is not None else 0
    ck = min(ATT_CK, n_lat) if n_lat else 0
    lk = n_ctx + n_lat
    head = lambda bi, h, i: (bi, 0, h)
    in_specs = [pl.BlockSpec((4, DA_HD), lambda bi, h, i: (0, 0)),
                pl.BlockSpec((1, DA_VD), lambda bi, h, i: (0, 0)),
                pl.BlockSpec((1, rows, DA_VD), lambda bi, h, i: (bi, i, h)),
                pl.BlockSpec((1, n_ctx, DA_VD), head), pl.BlockSpec((1, n_ctx, DA_VD), head)]
    args = [lam_p, subln_g.reshape(1, DA_VD), q, kv_ctx[0], kv_ctx[1]]
    if n_lat:
        in_specs += [pl.BlockSpec((1, n_lat, DA_VD), head)] * 2
        args += [kv_lat[0], kv_lat[1]]
    big = pltpu.VMEM((2, tq, lk), f32)
    small = pltpu.VMEM((tq, DA_VD), f32)
    pair = pltpu.VMEM((2, tq, DA_VD), f32)
    return pl.pallas_call(
        functools.partial(_attn_body, nsub, tq, ck, n_lat, lam_init),
        grid=(b, DA_HEADS, lq // rows),
        in_specs=in_specs,
        out_specs=pl.BlockSpec((1, rows, DA_VD), lambda bi, h, i: (bi, i, h)),
        out_shape=jax.ShapeDtypeStruct((b, lq, DA_W), bf16),
        scratch_shapes=[big, big, big, big, pair, pair, pair, small, small, small],
        compiler_params=_params("parallel", "parallel", "parallel"),
        name="diff_attn_lat" if n_lat else "diff_attn_ctx",
    )(*args)


def _outproj_body(hy_ref, cf_ref, da_ref, w_ref, x_ref, g2_ref, gn_ref, sh_ref, sc_ref, wr_hi_ref, wr_lo_ref,
                  br_ref, tri_ref, xo_ref, rows_ref, bucket_ref, rank_ref, cnt_ref, carry_ref):
    first = (pl.program_id(0) == 0) & (pl.program_id(1) == 0)

    @pl.when(first)
    def _():
        carry_ref[...] = jnp.zeros(carry_ref.shape, f32)

    y = (_dot(hy_ref[0], w_ref[0:HY_W, :]) + _dot(cf_ref[0], w_ref[HY_W:HY_W + CF_W, :])
         + _dot(da_ref[0], w_ref[HY_W + CF_W:MIX_W, :]))
    x = x_ref[0] + g2_ref[0] * y
    xo_ref[0] = x
    ms = jnp.mean(x * x, axis=-1, keepdims=True)
    h = x * lax.rsqrt(ms + EPS) * gn_ref[...]
    h = h * (1.0 + sc_ref[0]) + sh_ref[0]
    rows_ref[0, :, 0:D_MODEL] = h
    tm = h.shape[0]

    h_hi, h_lo = _split_bf16(h)
    logits = _dot_nt(wr_hi_ref[...], h_hi) + _dot_nt(wr_hi_ref[...], h_lo) + _dot_nt(wr_lo_ref[...], h_hi)
    scores = jax.nn.sigmoid(logits)
    sel = scores + br_ref[...]
    srow = [sel[e:e + 1, :] for e in range(N_EXPERTS)]
    crow = [scores[e:e + 1, :] for e in range(N_EXPERTS)]

    best = None
    for g in range(N_GROUPS):
        a, b, c, d = srow[EXPERTS_PER_GROUP * g:EXPERTS_PER_GROUP * (g + 1)]
        hi1, lo1, hi2, lo2 = jnp.maximum(a, b), jnp.minimum(a, b), jnp.maximum(c, d), jnp.minimum(c, d)
        gs = jnp.maximum(hi1, hi2) + jnp.maximum(jnp.minimum(hi1, hi2), jnp.maximum(lo1, lo2))
        if best is None:
            best, gb = gs, jnp.zeros(gs.shape, i32)
        else:
            better = gs > best
            best = jnp.where(better, gs, best)
            gb = jnp.where(better, g, gb)

    def pick(rows, j):
        out = rows[j]
        for g in range(1, N_GROUPS):
            out = jnp.where(gb == g, rows[EXPERTS_PER_GROUP * g + j], out)
        return out

    v = [pick(srow, j) for j in range(EXPERTS_PER_GROUP)]
    sc = [pick(crow, j) for j in range(EXPERTS_PER_GROUP)]

    def argmax_first(vals):
        idx, m = jnp.zeros(vals[0].shape, i32), vals[0]
        for j in range(1, len(vals)):
            better = vals[j] > m
            idx = jnp.where(better, j, idx)
            m = jnp.where(better, vals[j], m)
        return idx

    i1 = argmax_first(v)
    i2 = argmax_first([jnp.where(i1 == j, -jnp.inf, v[j]) for j in range(EXPERTS_PER_GROUP)])
    s1 = functools.reduce(lambda p, q: p + q, [jnp.where(i1 == j, sc[j], 0.0) for j in range(EXPERTS_PER_GROUP)])
    s2 = functools.reduce(lambda p, q: p + q, [jnp.where(i2 == j, sc[j], 0.0) for j in range(EXPERTS_PER_GROUP)])
    inv = 1.0 / (s1 + s2)
    gate = [jnp.where(i1 == j, s1 * inv, 0.0) + jnp.where(i2 == j, s2 * inv, 0.0) for j in range(EXPERTS_PER_GROUP)]

    lo, hi = jnp.minimum(i1, i2), jnp.maximum(i1, i2)
    pair = jnp.where(lo == 0, hi - 1, jnp.where(lo == 1, hi + 1, N_PAIRS - 1))
    bucket = gb * N_PAIRS + pair
    g_lo = functools.reduce(lambda p, q: p + q, [jnp.where(lo == j, gate[j], 0.0) for j in range(EXPERTS_PER_GROUP)])
    g_hi = functools.reduce(lambda p, q: p + q, [jnp.where(hi == j, gate[j], 0.0) for j in range(EXPERTS_PER_GROUP)])

    memb = jnp.concatenate([(bucket == k).astype(f32) for k in range(N_BUCKETS)]
                           + [jnp.zeros((BUCKET_ROWS - N_BUCKETS, tm), f32)], axis=0)
    before = _dot(memb.astype(bf16), tri_ref[...])
    rank = jnp.sum(memb * (before + carry_ref[:, 0:1]), axis=0, keepdims=True)
    carry_ref[...] = carry_ref[...] + jnp.sum(memb, axis=1, keepdims=True)
    cnt_ref[...] = carry_ref[...]
    bucket_ref[0, 0] = bucket
    rank_ref[0, 0] = rank.astype(i32)

    tail = jnp.concatenate([g_lo, g_hi, jnp.zeros((MOE_TAIL - 2, tm), f32)], axis=0)
    rows_ref[0, :, D_MODEL:MOE_ROW] = tail.T


def _outproj_router(hy, cf, da, w_out_bf, x, gate2, norm2_g, shift2, scale2, wrt_hi, wrt_lo, b_router_col):
    b, l, d = x.shape
    tm = min(512, l)
    row = lambda bi, i: (bi, i, 0)
    per_b = lambda bi, i: (bi, 0, 0)
    const = lambda bi, i: (0, 0)
    mod = pl.BlockSpec((1, 1, d), per_b)
    tri = (jnp.arange(tm, dtype=i32)[:, None] < jnp.arange(tm, dtype=i32)[None, :]).astype(bf16)
    idx_spec = pl.BlockSpec((1, 1, 1, tm), lambda bi, i: (bi, i, 0, 0))
    idx_shape = jax.ShapeDtypeStruct((b, l // tm, 1, tm), i32)
    xo, rows, bucket, rank, cnt = pl.pallas_call(
        _outproj_body,
        grid=(b, l // tm),
        in_specs=[pl.BlockSpec((1, tm, HY_W), row), pl.BlockSpec((1, tm, CF_W), row), pl.BlockSpec((1, tm, DA_W), row),
                  pl.BlockSpec((MIX_W, d), const), pl.BlockSpec((1, tm, d), row), mod,
                  pl.BlockSpec((1, d), const), mod, mod,
                  pl.BlockSpec((N_EXPERTS, d), const), pl.BlockSpec((N_EXPERTS, d), const),
                  pl.BlockSpec((N_EXPERTS, 1), const), pl.BlockSpec((tm, tm), const)],
        out_specs=[pl.BlockSpec((1, tm, d), row), pl.BlockSpec((1, tm, MOE_ROW), row), idx_spec, idx_spec,
                   pl.BlockSpec((BUCKET_ROWS, DA_VD), const)],
        out_shape=[jax.ShapeDtypeStruct((b, l, d), f32), jax.ShapeDtypeStruct((b, l, MOE_ROW), f32), idx_shape, idx_shape,
                   jax.ShapeDtypeStruct((BUCKET_ROWS, DA_VD), f32)],
        scratch_shapes=[pltpu.VMEM((BUCKET_ROWS, DA_VD), f32)],
        compiler_params=_params("arbitrary", "arbitrary"),
        name="outproj_router",
    )(hy, cf, da, w_out_bf, x, gate2, norm2_g, shift2, scale2, wrt_hi, wrt_lo, b_router_col, tri)
    nt = b * l // tm
    return xo, rows.reshape(b * l, MOE_ROW), bucket.reshape(nt, 1, tm), rank.reshape(nt, 1, tm), cnt


def _row_copy(src_ref, src_row, dst_ref, dst_row, sem):
    return pltpu.make_async_copy(src_ref.at[pl.ds(src_row, 1)], dst_ref.at[pl.ds(dst_row, 1)], sem)


def _dispatch_body(slot_ref, rows_ref, xs_ref, sem):
    tm = rows_ref.shape[0]

    def issue(r, carry):
        _row_copy(rows_ref, r, xs_ref, slot_ref[0, 0, r], sem).start()
        return carry

    lax.fori_loop(0, tm, issue, 0, unroll=8)
    pltpu.make_async_copy(rows_ref, xs_ref.at[pl.ds(0, tm)], sem).wait()


def _dispatch(rows, slot, n_rows):
    n_tok = rows.shape[0]
    tm = slot.shape[2]
    return pl.pallas_call(
        _dispatch_body,
        grid=(n_tok // tm,),
        in_specs=[pl.BlockSpec((1, 1, tm), lambda i: (i, 0, 0), memory_space=pltpu.SMEM),
                  pl.BlockSpec((tm, MOE_ROW), lambda i: (i, 0))],
        out_specs=pl.BlockSpec(memory_space=pl.ANY),
        out_shape=jax.ShapeDtypeStruct((n_rows, MOE_ROW), f32),
        scratch_shapes=[pltpu.SemaphoreType.DMA],
        compiler_params=_params("arbitrary"),
        name="moe_dispatch",
    )(slot, rows)


def _ffn_body(oblk_ref, e0_ref, e1_ref, nrow_ref, xs_ref, w1_ref, w3_ref, w2_ref, ys_ref, acc_ref):
    i = pl.program_id(0)
    j = pl.program_id(1)
    nrow = nrow_ref[i]

    @pl.when(nrow > 0)
    def _():
        tmf = xs_ref.shape[0]
        valid = lax.broadcasted_iota(i32, (tmf, 1), 0) < nrow
        h = jnp.where(valid, xs_ref[:, 0:D_MODEL], 0.0).astype(bf16)
        tail = xs_ref[:, D_MODEL:MOE_ROW]
        lane = lax.broadcasted_iota(i32, tail.shape, 1)
        gate = jnp.sum(jnp.where(valid & (lane == j), tail, 0.0), axis=-1, keepdims=True)
        a = _dot(h, w1_ref[0])
        a = a * jax.nn.sigmoid(a) * _dot(h, w3_ref[0])
        y = gate * _dot(a.astype(bf16), w2_ref[0])

        @pl.when(j == 0)
        def _():
            acc_ref[...] = y

        @pl.when(j > 0)
        def _():
            acc_ref[...] += y

        @pl.when(j == TOP_K - 1)
        def _():
            ys_ref[...] = acc_ref[...]


def _ffn(xs, oblk, e0, e1, nrow, w1_bf, w3_bf, w2_bf, tmf):
    n_rows = xs.shape[0]
    d = D_MODEL
    wmap = lambda i, j, oblk, e0, e1, nrow: (jnp.where(j == 0, e0[i], e1[i]), 0, 0)
    grid_spec = pltpu.PrefetchScalarGridSpec(
        num_scalar_prefetch=4,
        grid=(n_rows // tmf, TOP_K),
        in_specs=[pl.BlockSpec((tmf, MOE_ROW), lambda i, j, oblk, e0, e1, nrow: (i, 0)),
                  pl.BlockSpec((1, d, D_EXPERT), wmap), pl.BlockSpec((1, d, D_EXPERT), wmap),
                  pl.BlockSpec((1, D_EXPERT, d), wmap)],
        out_specs=pl.BlockSpec((tmf, d), lambda i, j, oblk, e0, e1, nrow: (oblk[i], 0)),
        scratch_shapes=[pltpu.VMEM((tmf, d), f32)])
    return pl.pallas_call(
        _ffn_body,
        grid_spec=grid_spec,
        out_shape=jax.ShapeDtypeStruct((n_rows + tmf, d), f32),
        compiler_params=_params("arbitrary", "arbitrary"),
        name="moe_ffn",
    )(oblk, e0, e1, nrow, xs, w1_bf, w3_bf, w2_bf)


def _combine_body(slot_ref, ys_ref, x_ref, g5_ref, o_ref, buf_ref, sem):
    tm = buf_ref.shape[0]

    def issue(r, carry):
        _row_copy(ys_ref, slot_ref[0, 0, r], buf_ref, r, sem).start()
        return carry

    lax.fori_loop(0, tm, issue, 0, unroll=8)
    pltpu.make_async_copy(ys_ref.at[pl.ds(0, tm)], buf_ref, sem).wait()
    o_ref[0] = x_ref[0] + g5_ref[0] * buf_ref[...]


def _combine(ys, slot, x, gate5):
    b, l, d = x.shape
    tm = slot.shape[2]
    nt = l // tm
    return pl.pallas_call(
        _combine_body,
        grid=(b, nt),
        in_specs=[pl.BlockSpec((1, 1, tm), lambda bi, i: (bi * nt + i, 0, 0), memory_space=pltpu.SMEM),
                  pl.BlockSpec(memory_space=pl.ANY),
                  pl.BlockSpec((1, tm, d), lambda bi, i: (bi, i, 0)),
                  pl.BlockSpec((1, 1, d), lambda bi, i: (bi, 0, 0))],
        out_specs=pl.BlockSpec((1, tm, d), lambda bi, i: (bi, i, 0)),
        out_shape=jax.ShapeDtypeStruct((b, l, d), f32),
        scratch_shapes=[pltpu.VMEM((tm, d), f32), pltpu.SemaphoreType.DMA],
        compiler_params=_params("arbitrary", "arbitrary"),
        name="moe_combine",
    )(slot, ys, x, gate5)


def _tile_tables(cnt, tmf, n_tiles):
    counts = cnt[:N_BUCKETS, 0].astype(i32)
    tiles = (counts + tmf - 1) // tmf
    ends = jnp.cumsum(tiles)
    starts = ends - tiles
    i = jnp.arange(n_tiles, dtype=i32)
    bkt = jnp.minimum(jnp.sum((i[:, None] >= ends[None, :]).astype(i32), axis=1), N_BUCKETS - 1)
    k = i - starts[bkt]
    used = i < ends[-1]
    oblk = jnp.where(used, i, n_tiles)
    nrow = jnp.where(used, jnp.clip(counts[bkt] - k * tmf, 0, tmf), 0)
    first = jnp.asarray(PAIR_LO, i32)[bkt % N_PAIRS] + EXPERTS_PER_GROUP * (bkt // N_PAIRS)
    second = jnp.asarray(PAIR_HI, i32)[bkt % N_PAIRS] + EXPERTS_PER_GROUP * (bkt // N_PAIRS)
    return starts * tmf, oblk.astype(i32), first.astype(i32), second.astype(i32), nrow.astype(i32)


def _moe(rows, bucket, rank, cnt, x, gate5, w1_bf, w3_bf, w2_bf):
    n_tok = rows.shape[0]
    tmf = min(MOE_TMF, n_tok)
    n_tiles = n_tok // tmf + N_BUCKETS
    first_row, oblk, e0, e1, nrow = _tile_tables(cnt, tmf, n_tiles)
    slot = first_row[bucket] + rank
    xs = _dispatch(rows, slot, n_tiles * tmf)
    ys = _ffn(xs, oblk, e0, e1, nrow, w1_bf, w3_bf, w2_bf, tmf)
    return _combine(ys, slot, x, gate5)


def _rope_tables(n_lat):
    rows = n_lat // GRID_W
    row = jnp.repeat(jnp.arange(rows, dtype=f32), GRID_W)
    col = jnp.tile(jnp.arange(GRID_W, dtype=f32), rows)
    inv = ROPE_THETA ** (-jnp.arange(ROPE_AXIS_FREQS, dtype=f32) / ROPE_AXIS_FREQS)
    ang = jnp.concatenate([row[:, None] * inv, col[:, None] * inv], axis=-1)
    cos, sin = jnp.cos(ang), jnp.sin(ang)
    reps = DA_QK // DA_HD
    return jnp.tile(jnp.concatenate([cos, cos], -1), (1, reps)), jnp.tile(jnp.concatenate([-sin, sin], -1), (1, reps))


def _dft_tables(l):
    n = 2 * l
    f = jnp.arange(l, dtype=i32)[:, None]
    t = jnp.arange(l, dtype=i32)[None, :]
    ang = ((f * t) % n).astype(f32) * (2.0 * math.pi / n)
    cm = jnp.cos(ang)
    sm = jnp.where(f == 0, jnp.where(t % 2 == 0, 1.0, -1.0), jnp.sin(ang))
    return cm.astype(bf16), sm.astype(bf16), sm.T.astype(bf16)


def _hy_features(l):
    t_idx = jnp.arange(l, dtype=f32)[:, None]
    t01 = t_idx / max(l - 1, 1)
    bands = jnp.linspace(1e-4, HY_BANDS - 1, HY_BANDS, dtype=f32)
    ang = 2.0 * math.pi * bands * t_idx / l
    feats = jnp.concatenate([t01, jnp.cos(ang), -jnp.sin(ang)], axis=-1)
    return jnp.pad(feats, ((0, 0), (0, HY_EMB_PAD - HY_EMB)))


def _hyena(u_hy, dft, feats, deltas, conv_w, conv_b, w1p, b1, w2, b2, w3, freq, bias):
    cm, sm, smt = dft
    x0, z = _hyprep(u_hy, conv_w, conv_b)
    hs, hd = _hy_filter(feats, w1p, b1, w2, b2, w3, freq, deltas)
    a, bco, dco = _hy_spectrum(cm, sm, hs, hd)
    return _fftconv(z, x0, cm, sm, smt, a, bco, dco, bias)


def kernel(x, c, ctx, c_ctx, w_ada, b_ada, norm1_g, norm2_g, w_in, hy_conv_w, hy_conv_b, hy_w1, hy_b1, hy_w2, hy_b2,
           hy_w3, hy_freq, hy_bias, cf_dw_w, cf_dw_b, cf_ln_g, cf_ln_b, da_qn_g, da_kn_g, da_lam, da_subln_g, w_out,
           w_router, b_router, moe_w1, moe_w3, moe_w2):
    depth = w_ada.shape[0]
    bsz, n_lat, d = x.shape
    n_ctx = ctx.shape[1]

    rope = _rope_tables(n_lat)
    dft_lat, dft_ctx = _dft_tables(n_lat), _dft_tables(n_ctx)
    feats_lat, feats_ctx = _hy_features(n_lat), _hy_features(n_ctx)
    deltas = jnp.abs(jnp.linspace(HY_MIN_DECAY, HY_MAX_DECAY, HY_W, dtype=f32)).reshape(1, HY_W)
    seg = jnp.arange(DA_QK, dtype=i32) // DA_HD
    bd = (seg[:, None] == seg[None, :]).astype(bf16)

    rows = jnp.concatenate([c, c_ctx[None], jnp.zeros((7, d), f32)], axis=0)
    mod_all = _ada(rows, w_ada, b_ada)

    wr_hi = w_router.T.astype(bf16)
    wr_lo = (w_router.T - wr_hi.astype(f32)).astype(bf16)
    br = b_router.reshape(N_EXPERTS, 1)
    hy_w1p = jnp.pad(hy_w1, ((0, 0), (0, HY_EMB_PAD - HY_EMB), (0, 0)))

    x_lat, x_ctx = x, ctx
    for l in range(depth):
        last = l == depth - 1
        lam_init = 0.8 - 0.6 * math.exp(-0.3 * l)
        mod = mod_all[l, :bsz].reshape(bsz, 6, 1, d)
        mod_c = jnp.broadcast_to(mod_all[l, bsz].reshape(1, 6, 1, d), (bsz, 6, 1, d))
        w_in_bf = w_in[l].astype(bf16)
        w_out_bf = w_out[l].astype(bf16)
        w1_bf, w3_bf, w2_bf = moe_w1[l].astype(bf16), moe_w3[l].astype(bf16), moe_w2[l].astype(bf16)
        g1 = norm1_g[l].reshape(1, d)
        g2 = norm2_g[l].reshape(1, d)
        qg = jnp.tile(da_qn_g[l], DA_QK // DA_HD).reshape(1, DA_QK)
        kg = jnp.tile(da_kn_g[l], DA_QK // DA_HD).reshape(1, DA_QK)
        hy_p = (hy_conv_w[l], hy_conv_b[l], hy_w1p[l], hy_b1[l], hy_w2[l], hy_b2[l], hy_w3[l], hy_freq[l], hy_bias[l])
        cf_p = (cf_dw_w[l], cf_dw_b[l], cf_ln_g[l], cf_ln_b[l])

        uhy, ucf, q_lat, k_lat, v_lat = _inproj(x_lat, mod[:, 0], mod[:, 1], g1, w_in_bf, qg, kg, bd, rope)
        uhy_c, ucf_c, q_ctx, k_ctx, v_ctx = _inproj(x_ctx, mod_c[:, 0], mod_c[:, 1], g1, w_in_bf, qg, kg, bd, None)
        hy_lat = _hyena(uhy, dft_lat, feats_lat, deltas, *hy_p)
        cf_lat = _conformer(ucf, *cf_p)
        da_lat = _diff_attn(q_lat, (k_ctx, v_ctx), (k_lat, v_lat), da_lam[l], da_subln_g[l], lam_init)
        x_lat, *routed = _outproj_router(hy_lat, cf_lat, da_lat, w_out_bf, x_lat, mod[:, 2], g2, mod[:, 3], mod[:, 4],
                                         wr_hi, wr_lo, br)
        x_lat = _moe(*routed, x_lat, mod[:, 5], w1_bf, w3_bf, w2_bf)
        if not last:
            hy_c = _hyena(uhy_c, dft_ctx, feats_ctx, deltas, *hy_p)
            cf_c = _conformer(ucf_c, *cf_p)
            da_c = _diff_attn(q_ctx, (k_ctx, v_ctx), None, da_lam[l], da_subln_g[l], lam_init)
            x_ctx, *routed_c = _outproj_router(hy_c, cf_c, da_c, w_out_bf, x_ctx, mod_c[:, 2], g2, mod_c[:, 3],
                                               mod_c[:, 4], wr_hi, wr_lo, br)
            x_ctx = _moe(*routed_c, x_ctx, mod_c[:, 5], w1_bf, w3_bf, w2_bf)
    return x_lat
```

```python
import functools
import math

import jax
import jax.numpy as jnp
from jax import lax
from jax.experimental import pallas as pl
from jax.experimental.pallas import tpu as pltpu

f32 = jnp.float32
bf16 = jnp.bfloat16
i32 = jnp.int32

D_MODEL = 1024
GRID_W = 64
HY_W = 256
CF_W = 256
DA_HEADS = 4
DA_HD = 64
DA_VD = 2 * DA_HD
DA_W = DA_HEADS * DA_VD
MIX_W = HY_W + CF_W + DA_W
HY_IN = 3 * HY_W
CF_IN = 2 * CF_W
DA_QK = DA_HEADS * 2 * DA_HD
OFF_CF = HY_IN
OFF_Q = OFF_CF + CF_IN
OFF_K = OFF_Q + DA_QK
OFF_V = OFF_K + DA_QK
IN_W = OFF_V + DA_W
CF_CONV = 31
HY_BANDS = 16
HY_EMB = 1 + 2 * HY_BANDS
HY_EMB_PAD = 64
HY_FF = 64
HY_MIN_DECAY = math.log(1e-2) / 1.5
HY_MAX_DECAY = math.log(1e-2) / 0.3
ROPE_THETA = 10000.0
ROPE_AXIS_FREQS = DA_HD // 4
N_EXPERTS = 16
N_GROUPS = 4
EXPERTS_PER_GROUP = N_EXPERTS // N_GROUPS
D_EXPERT = 512
EPS = 1e-6

VMEM_LIMIT_BYTES = 56 * 1024 * 1024
CF_HALO = 16
BF16_ROWS = 16
ATT_TQ = 256
ATT_NSUB = 8
ATT_CK = 512
LOG2E = 1.4426950408889634
MOE_TAIL = 128
MOE_ROW = D_MODEL + MOE_TAIL
MOE_TMF = 512
TOP_K = 2
PAIR_LO = (0, 0, 0, 1, 1, 2)
PAIR_HI = (1, 2, 3, 2, 3, 3)
N_PAIRS = len(PAIR_LO)
N_BUCKETS = N_GROUPS * N_PAIRS
BUCKET_ROWS = 32


def _params(*sem):
    return pltpu.CompilerParams(dimension_semantics=sem, vmem_limit_bytes=VMEM_LIMIT_BYTES)


def _dot(a, b):
    return jnp.dot(a, b, preferred_element_type=f32)


def _dot_nt(a, b):
    return lax.dot_general(a, b, (((1,), (1,)), ((), ())), preferred_element_type=f32)


def _split_bf16(v):
    hi = v.astype(bf16)
    lo = (v - hi.astype(f32)).astype(bf16)
    return hi, lo


def _ada_body(c_ref, w_ref, b_ref, o_ref):
    c = c_ref[...]
    s = c * jax.nn.sigmoid(c)
    o_ref[0] = _dot(s, w_ref[0]) + b_ref[0]


def _ada(cc, w_ada, b_ada):
    depth, d, n = w_ada.shape
    r = cc.shape[0]
    tn = 1536
    return pl.pallas_call(
        _ada_body,
        grid=(depth, n // tn),
        in_specs=[pl.BlockSpec((r, d), lambda l, j: (0, 0)),
                  pl.BlockSpec((1, d, tn), lambda l, j: (l, 0, j)),
                  pl.BlockSpec((1, 1, tn), lambda l, j: (l, 0, j))],
        out_specs=pl.BlockSpec((1, r, tn), lambda l, j: (l, 0, j)),
        out_shape=jax.ShapeDtypeStruct((depth, r, n), f32),
        compiler_params=_params("parallel", "parallel"),
        name="ada_mod",
    )(cc, w_ada, b_ada.reshape(depth, 1, n))


def _inproj_body(rope, x_ref, sh_ref, sc_ref, g_ref, w_ref, qg_ref, kg_ref, bd_ref, *rest):
    if rope:
        cos_ref, sin_ref, hy_ref, cf_ref, q_ref, k_ref, v_ref = rest
    else:
        hy_ref, cf_ref, q_ref, k_ref, v_ref = rest
    x = x_ref[0]
    ms = jnp.mean(x * x, axis=-1, keepdims=True)
    h = x * lax.rsqrt(ms + EPS) * g_ref[...]
    h = h * (1.0 + sc_ref[0]) + sh_ref[0]
    hb = h.astype(bf16)

    def proj(lo, hi):
        return _dot(hb, w_ref[:, lo:hi])

    hy_ref[0] = proj(0, OFF_CF).astype(bf16)
    cf_ref[0] = proj(OFF_CF, OFF_Q).astype(bf16)
    v_ref[0] = proj(OFF_V, IN_W).astype(bf16)

    lane = lax.broadcasted_iota(i32, (1, DA_QK), 1)
    first_half = (lane % DA_HD) < (DA_HD // 2)

    def qk_heads(lo, hi, gain_ref, out_ref, scale):
        t = proj(lo, hi)
        ss = _dot((t * t).astype(bf16), bd_ref[...])
        tn = t * lax.rsqrt(ss * (1.0 / DA_HD) + EPS) * gain_ref[...]
        if rope:
            half = DA_HD // 2
            partner = jnp.where(first_half, pltpu.roll(tn, DA_QK - half, 1), pltpu.roll(tn, half, 1))
            tn = tn * cos_ref[...] + partner * sin_ref[...]
        out_ref[0] = (tn * scale).astype(bf16)

    qk_heads(OFF_Q, OFF_K, qg_ref, q_ref, LOG2E * DA_HD ** -0.5)
    qk_heads(OFF_K, OFF_V, kg_ref, k_ref, 1.0)


def _inproj(x, shift, scale, g, w_bf, qg, kg, bd, rope_tabs):
    b, l, d = x.shape
    tm = min(512, l)
    rope = rope_tabs is not None
    row = lambda bi, i: (bi, i, 0)
    per_b = lambda bi, i: (bi, 0, 0)
    const = lambda bi, i: (0, 0)
    in_specs = [pl.BlockSpec((1, tm, d), row),
                pl.BlockSpec((1, 1, d), per_b),
                pl.BlockSpec((1, 1, d), per_b),
                pl.BlockSpec((1, d), const),
                pl.BlockSpec((d, IN_W), const),
                pl.BlockSpec((1, DA_QK), const),
                pl.BlockSpec((1, DA_QK), const),
                pl.BlockSpec((DA_QK, DA_QK), const)]
    args = [x, shift, scale, g, w_bf, qg, kg, bd]
    if rope:
        in_specs += [pl.BlockSpec((tm, DA_QK), lambda bi, i: (i, 0))] * 2
        args += list(rope_tabs)
    widths = (HY_IN, CF_IN, DA_QK, DA_QK, DA_W)
    return pl.pallas_call(
        functools.partial(_inproj_body, rope),
        grid=(b, l // tm),
        in_specs=in_specs,
        out_specs=[pl.BlockSpec((1, tm, w), row) for w in widths],
        out_shape=[jax.ShapeDtypeStruct((b, l, w), bf16) for w in widths],
        compiler_params=_params("parallel", "parallel"),
        name="inproj_rope" if rope else "inproj_ctx",
    )(*args)


def _hyprep_body(u_ref, w_ref, b_ref, x0_ref, z_ref):
    l = u_ref.shape[1]
    t = min(128, l)
    n = l // t
    row = lax.broadcasted_iota(i32, (t, 1), 0)

    def chunk(i, carry):
        t0 = pl.multiple_of(i * t, t)
        tp = pl.multiple_of(jnp.maximum(t0 - BF16_ROWS, 0), BF16_ROWS)
        tx = pl.multiple_of(jnp.minimum(t0 + t, l - BF16_ROWS), BF16_ROWS)
        outs = []
        for gi in range(3):
            ls = slice(gi * HY_W, (gi + 1) * HY_W)
            a = u_ref[0, pl.ds(t0, t), ls].astype(f32)
            prev = u_ref[0, pl.ds(tp, BF16_ROWS), ls].astype(f32)[BF16_ROWS - 1:BF16_ROWS]
            prev = jnp.where(i > 0, prev, 0.0)
            nxt = u_ref[0, pl.ds(tx, BF16_ROWS), ls].astype(f32)[0:1]
            nxt = jnp.where(i < n - 1, nxt, 0.0)
            um = jnp.where(row == 0, prev, pltpu.roll(a, 1, 0))
            up = jnp.where(row == t - 1, nxt, pltpu.roll(a, t - 1, 0))
            w = w_ref[:, ls]
            outs.append(w[0:1] * um + w[1:2] * a + w[2:3] * up + b_ref[:, ls])
        x0, x1, v = outs
        x0_ref[0, pl.ds(t0, t), :] = x0.astype(bf16)
        z_ref[0, pl.ds(t0, t), :] = (x1 * v).astype(bf16)
        return carry

    lax.fori_loop(0, n, chunk, 0)


def _hyprep(u_hy, conv_w, conv_b):
    b, l, _ = u_hy.shape
    per_b = lambda bi: (bi, 0, 0)
    const = lambda bi: (0, 0)
    return pl.pallas_call(
        _hyprep_body,
        grid=(b,),
        in_specs=[pl.BlockSpec((1, l, HY_IN), per_b),
                  pl.BlockSpec((3, HY_IN), const),
                  pl.BlockSpec((1, HY_IN), const)],
        out_specs=[pl.BlockSpec((1, l, HY_W), per_b)] * 2,
        out_shape=[jax.ShapeDtypeStruct((b, l, HY_W), bf16)] * 2,
        compiler_params=_params("parallel"),
        name="hyena_prep",
    )(u_hy, conv_w, conv_b.reshape(1, HY_IN))


def _filter_body(feat_ref, w1_ref, b1_ref, w2_ref, b2_ref, w3_ref, fr_ref, dl_ref, hs_ref, hd_ref):
    tl = feat_ref.shape[0]
    feats = feat_ref[...]
    fr = fr_ref[...]
    hid = jnp.sin(fr * (_dot(feats, w1_ref[...]) + b1_ref[...]))
    hid = jnp.sin(fr * (_dot(hid, w2_ref[...]) + b2_ref[...]))
    h = _dot(hid, w3_ref[...])
    win = jnp.exp(-feats[:, 0:1] * dl_ref[...])
    fwd = h[:, :HY_W] * win
    bwd = h[:, HY_W:] * win
    row = pl.program_id(0) * tl + lax.broadcasted_iota(i32, (tl, 1), 0)
    bwd = jnp.where(row == 0, 0.0, bwd)
    hs_ref[...] = fwd + bwd
    hd_ref[...] = bwd - fwd


def _hy_filter(feats, w1p, b1, w2, b2, w3, freq, deltas):
    l = feats.shape[0]
    tl = min(512, l)
    const = lambda i: (0, 0)
    return pl.pallas_call(
        _filter_body,
        grid=(l // tl,),
        in_specs=[pl.BlockSpec((tl, HY_EMB_PAD), lambda i: (i, 0)),
                  pl.BlockSpec((HY_EMB_PAD, HY_FF), const),
                  pl.BlockSpec((1, HY_FF), const),
                  pl.BlockSpec((HY_FF, HY_FF), const),
                  pl.BlockSpec((1, HY_FF), const),
                  pl.BlockSpec((HY_FF, 2 * HY_W), const),
                  pl.BlockSpec((1, HY_FF), const),
                  pl.BlockSpec((1, HY_W), const)],
        out_specs=[pl.BlockSpec((tl, HY_W), lambda i: (i, 0))] * 2,
        out_shape=[jax.ShapeDtypeStruct((l, HY_W), f32)] * 2,
        compiler_params=_params("parallel"),
        name="hyena_filter",
    )(feats, w1p, b1.reshape(1, HY_FF), w2, b2.reshape(1, HY_FF), w3, freq.reshape(1, HY_FF), deltas)


def _kf_body(inv_n, c_ref, s_ref, hs_ref, hd_ref, a_ref, b_ref, d_ref):
    tf = c_ref.shape[0]
    l = hs_ref.shape[0]
    hs = hs_ref[...]
    hs_hi, hs_lo = _split_bf16(hs)
    hd_hi, hd_lo = _split_bf16(hd_ref[...])
    kre = _dot(c_ref[...], hs_hi) + _dot(c_ref[...], hs_lo)
    kim = _dot(s_ref[...], hd_hi) + _dot(s_ref[...], hd_lo)
    tpos = lax.broadcasted_iota(i32, (l, 1), 0)
    knyq = jnp.sum(jnp.where(tpos % 2 == 0, hs, -hs), axis=0, keepdims=True)
    row = pl.program_id(0) * tf + lax.broadcasted_iota(i32, (tf, 1), 0)
    dc = row == 0
    a_ref[...] = jnp.where(dc, kre * inv_n, kre * (2.0 * inv_n))
    b_ref[...] = jnp.where(dc, 0.0, kim * (2.0 * inv_n))
    d_ref[...] = jnp.where(dc, knyq * inv_n, kre * (2.0 * inv_n))


def _hy_spectrum(cm, sm, hs, hd):
    l = hs.shape[0]
    tf = min(512, l)
    full = lambda i: (0, 0)
    rows = lambda i: (i, 0)
    return pl.pallas_call(
        functools.partial(_kf_body, 1.0 / (2 * l)),
        grid=(l // tf,),
        in_specs=[pl.BlockSpec((tf, l), rows), pl.BlockSpec((tf, l), rows),
                  pl.BlockSpec((l, HY_W), full), pl.BlockSpec((l, HY_W), full)],
        out_specs=[pl.BlockSpec((tf, HY_W), rows)] * 3,
        out_shape=[jax.ShapeDtypeStruct((l, HY_W), f32)] * 3,
        compiler_params=_params("parallel"),
        name="hyena_spectrum",
    )(cm, sm, hs, hd)


def _fftconv_body(z_ref, x0_ref, cr_ref, sr_ref, cc_ref, sc_ref, a_ref, b_ref, d_ref, bias_ref, o_ref, acc_ref):
    j = pl.program_id(1)
    bco = b_ref[...]
    for bi in range(z_ref.shape[0]):
        z = z_ref[bi]
        zre = _dot(cr_ref[...], z)
        zs = _dot(sr_ref[...], z)
        yre = (a_ref[...] * zre + bco * zs).astype(bf16)
        yim = (d_ref[...] * zs - bco * zre).astype(bf16)
        part = _dot(cc_ref[...], yre) + _dot(sc_ref[...], yim)

        @pl.when(j == 0)
        def _():
            acc_ref[bi] = part

        @pl.when(j > 0)
        def _():
            acc_ref[bi] += part

        @pl.when(j == pl.num_programs(1) - 1)
        def _():
            y = acc_ref[bi] + bias_ref[...] * z.astype(f32)
            o_ref[bi] = (x0_ref[bi].astype(f32) * y).astype(bf16)


def _fftconv(z, x0, cm, sm, smt, a, bco, dco, bias):
    b, l, _ = z.shape
    tf = min(256, l)
    nb = 1
    per_b = lambda bi, j: (bi, 0, 0)
    frow = lambda bi, j: (j, 0)
    fcol = lambda bi, j: (0, j)
    return pl.pallas_call(
        _fftconv_body,
        grid=(b // nb, l // tf),
        in_specs=[pl.BlockSpec((nb, l, HY_W), per_b), pl.BlockSpec((nb, l, HY_W), per_b),
                  pl.BlockSpec((tf, l), frow), pl.BlockSpec((tf, l), frow),
                  pl.BlockSpec((l, tf), fcol), pl.BlockSpec((l, tf), fcol),
                  pl.BlockSpec((tf, HY_W), frow), pl.BlockSpec((tf, HY_W), frow), pl.BlockSpec((tf, HY_W), frow),
                  pl.BlockSpec((1, HY_W), lambda bi, j: (0, 0))],
        out_specs=pl.BlockSpec((nb, l, HY_W), per_b),
        out_shape=jax.ShapeDtypeStruct((b, l, HY_W), bf16),
        scratch_shapes=[pltpu.VMEM((nb, l, HY_W), f32)],
        compiler_params=_params("parallel", "arbitrary"),
        name="hyena_fftconv",
    )(z, x0, cm, sm, cm, smt, a, bco, dco, bias.reshape(1, HY_W))


def _conformer_body(u_ref, up_ref, un_ref, w_ref, b_ref, g_ref, beta_ref, o_ref, ext_ref):
    i = pl.program_id(1)
    tc = u_ref.shape[1]
    t = min(128, tc)

    def glu(ref, rows):
        a = ref[0, rows, 0:CF_W].astype(f32)
        g = ref[0, rows, CF_W:CF_IN].astype(f32)
        return a * jax.nn.sigmoid(g)

    ext_ref[0:CF_HALO, :] = jnp.where(i > 0, glu(up_ref, slice(None)), 0.0)
    ext_ref[CF_HALO + tc:CF_HALO + tc + CF_HALO, :] = jnp.where(i < pl.num_programs(1) - 1, glu(un_ref, slice(None)), 0.0)
    for t0 in range(0, tc, t):
        ext_ref[CF_HALO + t0:CF_HALO + t0 + t, :] = glu(u_ref, slice(t0, t0 + t))

    for t0 in range(0, tc, t):
        acc = jnp.zeros((t, CF_W), f32) + b_ref[...]
        for j in range(CF_CONV):
            lo = t0 + CF_HALO - CF_CONV // 2 + j
            acc = acc + w_ref[j:j + 1, :] * ext_ref[lo:lo + t, :]
        mu = jnp.mean(acc, axis=-1, keepdims=True)
        xc = acc - mu
        var = jnp.mean(xc * xc, axis=-1, keepdims=True)
        y = xc * lax.rsqrt(var + EPS) * g_ref[...] + beta_ref[...]
        o_ref[0, t0:t0 + t, :] = (y * jax.nn.sigmoid(y)).astype(bf16)


def _conformer(u_cf, dw_w, dw_b, ln_g, ln_b):
    b, l, _ = u_cf.shape
    tc = min(512, l)
    hb = tc // CF_HALO
    nh = l // CF_HALO
    const = lambda bi, i: (0, 0)
    vec = pl.BlockSpec((1, CF_W), const)
    return pl.pallas_call(
        _conformer_body,
        grid=(b, l // tc),
        in_specs=[pl.BlockSpec((1, tc, CF_IN), lambda bi, i: (bi, i, 0)),
                  pl.BlockSpec((1, CF_HALO, CF_IN), lambda bi, i: (bi, jnp.maximum(i * hb - 1, 0), 0)),
                  pl.BlockSpec((1, CF_HALO, CF_IN), lambda bi, i: (bi, jnp.minimum((i + 1) * hb, nh - 1), 0)),
                  pl.BlockSpec((CF_CONV, CF_W), const), vec, vec, vec],
        out_specs=pl.BlockSpec((1, tc, CF_W), lambda bi, i: (bi, i, 0)),
        out_shape=jax.ShapeDtypeStruct((b, l, CF_W), bf16),
        scratch_shapes=[pltpu.VMEM((tc + 2 * CF_HALO, CF_W), f32)],
        compiler_params=_params("parallel", "parallel"),
        name="conformer_conv",
    )(u_cf, u_cf, u_cf, dw_w, dw_b.reshape(1, CF_W), ln_g.reshape(1, CF_W), ln_b.reshape(1, CF_W))


def _attn_body(nsub, tq, ck, n_lat, lam_init, lam_ref, g_ref, q_ref, kc_ref, vc_ref, *rest):
    if n_lat:
        kl_ref, vl_ref = rest[:2]
        rest = rest[2:]
    o_ref, s_a, s_b, mpart_ref, mrow_ref, lpart_ref, acc_ref = rest
    s_bufs = (s_a, s_b)
    n_ctx = kc_ref.shape[1]
    lane = lax.broadcasted_iota(i32, (1, DA_VD), 1)
    lp = lam_ref[...]
    lam = (jnp.exp(jnp.sum(lp[0:1] * lp[1:2], axis=-1, keepdims=True))
           - jnp.exp(jnp.sum(lp[2:3] * lp[3:4], axis=-1, keepdims=True)) + lam_init)

    def stage(t):
        do1, do2 = t < nsub, t >= 1
        s_w, s_r = s_bufs[t % 2], s_bufs[(t - 1) % 2]
        if do1:
            q = q_ref[0, t * tq:(t + 1) * tq, :]
            qm = [jnp.where(lane < DA_HD, q, jnp.zeros_like(q)), jnp.where(lane >= DA_HD, q, jnp.zeros_like(q))]
            mpart_ref[...] = jnp.full(mpart_ref.shape, -jnp.inf, f32)
        if do2:
            lpart_ref[...] = jnp.zeros(lpart_ref.shape, f32)
            acc_ref[...] = jnp.zeros(acc_ref.shape, f32)

        def chunk(k, vt, off, width):
            nb = width // DA_VD
            cols = [slice(off + j * DA_VD, off + (j + 1) * DA_VD) for j in range(nb)]
            if do1:
                for m in range(2):
                    s = _dot_nt(qm[m], k)
                    pm = s[:, 0:DA_VD]
                    for j in range(nb):
                        blk = s[:, j * DA_VD:(j + 1) * DA_VD]
                        s_w[m, :, cols[j]] = blk
                        if j:
                            pm = jnp.maximum(pm, blk)
                    mpart_ref[m] = jnp.maximum(mpart_ref[m], pm)
            if do2:
                for m in range(2):
                    mr = mrow_ref[m]
                    ls = lpart_ref[m]
                    parts = []
                    for j in range(nb):
                        e = jnp.exp2(s_r[m, :, cols[j]] - mr)
                        ls = ls + e
                        parts.append(e.astype(bf16))
                    lpart_ref[m] = ls
                    p = jnp.concatenate(parts, axis=1) if nb > 1 else parts[0]
                    acc_ref[m] += _dot_nt(vt, p)

        chunk(kc_ref[0], vc_ref[0], 0, n_ctx)
        for c in range(n_lat // ck if n_lat else 0):
            chunk(kl_ref[0, c * ck:(c + 1) * ck, :], vl_ref[0, :, c * ck:(c + 1) * ck], n_ctx + c * ck, ck)

        if do2:
            l0 = jnp.sum(lpart_ref[0], axis=-1, keepdims=True)
            l1 = jnp.sum(lpart_ref[1], axis=-1, keepdims=True)
            o = acc_ref[0].T * (1.0 / l0) - acc_ref[1].T * (lam / l1)
            ms = jnp.mean(o * o, axis=-1, keepdims=True)
            o_ref[0, (t - 1) * tq:t * tq, :] = (o * lax.rsqrt(ms + EPS) * g_ref[...] * (1.0 - lam_init)).astype(bf16)
        if do1:
            for m in range(2):
                mrow_ref[m] = jnp.broadcast_to(jnp.max(mpart_ref[m], axis=-1, keepdims=True), (tq, DA_VD))

    for t in range(nsub + 1):
        stage(t)


def _diff_attn(q, kv_ctx, kv_lat, lam_p, subln_g, lam_init):
    b, lq, _ = q.shape
    tq = min(ATT_TQ, lq)
    nsub = min(ATT_NSUB, lq // tq)
    rows = nsub * tq
    n_ctx = kv_ctx[0].shape[1]
    n_lat = kv_lat[0].shape[1] if kv_lat is not None else 0
    ck = min(ATT_CK, n_lat) if n_lat else 0
    lk = n_ctx + n_lat
    head = lambda bi, h, i: (bi, 0, h)
    head_t = lambda bi, h, i: (bi, h, 0)
    in_specs = [pl.BlockSpec((4, DA_HD), lambda bi, h, i: (0, 0)),
                pl.BlockSpec((1, DA_VD), lambda bi, h, i: (0, 0)),
                pl.BlockSpec((1, rows, DA_VD), lambda bi, h, i: (bi, i, h)),
                pl.BlockSpec((1, n_ctx, DA_VD), head), pl.BlockSpec((1, DA_VD, n_ctx), head_t)]
    args = [lam_p, subln_g.reshape(1, DA_VD), q, kv_ctx[0], jnp.swapaxes(kv_ctx[1], 1, 2)]
    if n_lat:
        in_specs += [pl.BlockSpec((1, n_lat, DA_VD), head), pl.BlockSpec((1, DA_VD, n_lat), head_t)]
        args += [kv_lat[0], jnp.swapaxes(kv_lat[1], 1, 2)]
    big = pltpu.VMEM((2, tq, lk), f32)
    pair = pltpu.VMEM((2, tq, DA_VD), f32)
    return pl.pallas_call(
        functools.partial(_attn_body, nsub, tq, ck, n_lat, lam_init),
        grid=(b, DA_HEADS, lq // rows),
        in_specs=in_specs,
        out_specs=pl.BlockSpec((1, rows, DA_VD), lambda bi, h, i: (bi, i, h)),
        out_shape=jax.ShapeDtypeStruct((b, lq, DA_W), bf16),
        scratch_shapes=[big, big, pair, pair, pair, pltpu.VMEM((2, DA_VD, tq), f32)],
        compiler_params=_params("parallel", "parallel", "parallel"),
        name="diff_attn_lat" if n_lat else "diff_attn_ctx",
    )(*args)


def _outproj_body(hy_ref, cf_ref, da_ref, w_ref, x_ref, g2_ref, gn_ref, sh_ref, sc_ref, wr_hi_ref, wr_lo_ref,
                  br_ref, tri_ref, xo_ref, rows_ref, bucket_ref, rank_ref, cnt_ref, carry_ref):
    first = (pl.program_id(0) == 0) & (pl.program_id(1) == 0)

    @pl.when(first)
    def _():
        carry_ref[...] = jnp.zeros(carry_ref.shape, f32)

    y = (_dot(hy_ref[0], w_ref[0:HY_W, :]) + _dot(cf_ref[0], w_ref[HY_W:HY_W + CF_W, :])
         + _dot(da_ref[0], w_ref[HY_W + CF_W:MIX_W, :]))
    x = x_ref[0] + g2_ref[0] * y
    xo_ref[0] = x
    ms = jnp.mean(x * x, axis=-1, keepdims=True)
    h = x * lax.rsqrt(ms + EPS) * gn_ref[...]
    h = h * (1.0 + sc_ref[0]) + sh_ref[0]
    rows_ref[0, :, 0:D_MODEL] = h
    tm = h.shape[0]

    h_hi, h_lo = _split_bf16(h)
    logits = _dot_nt(wr_hi_ref[...], h_hi) + _dot_nt(wr_hi_ref[...], h_lo) + _dot_nt(wr_lo_ref[...], h_hi)
    scores = jax.nn.sigmoid(logits)
    sel = scores + br_ref[...]
    srow = [sel[e:e + 1, :] for e in range(N_EXPERTS)]
    crow = [scores[e:e + 1, :] for e in range(N_EXPERTS)]

    best = None
    for g in range(N_GROUPS):
        a, b, c, d = srow[EXPERTS_PER_GROUP * g:EXPERTS_PER_GROUP * (g + 1)]
        hi1, lo1, hi2, lo2 = jnp.maximum(a, b), jnp.minimum(a, b), jnp.maximum(c, d), jnp.minimum(c, d)
        gs = jnp.maximum(hi1, hi2) + jnp.maximum(jnp.minimum(hi1, hi2), jnp.maximum(lo1, lo2))
        if best is None:
            best, gb = gs, jnp.zeros(gs.shape, i32)
        else:
            better = gs > best
            best = jnp.where(better, gs, best)
            gb = jnp.where(better, g, gb)

    def pick(rows, j):
        out = rows[j]
        for g in range(1, N_GROUPS):
            out = jnp.where(gb == g, rows[EXPERTS_PER_GROUP * g + j], out)
        return out

    v = [pick(srow, j) for j in range(EXPERTS_PER_GROUP)]
    sc = [pick(crow, j) for j in range(EXPERTS_PER_GROUP)]

    def argmax_first(vals):
        idx, m = jnp.zeros(vals[0].shape, i32), vals[0]
        for j in range(1, len(vals)):
            better = vals[j] > m
            idx = jnp.where(better, j, idx)
            m = jnp.where(better, vals[j], m)
        return idx

    i1 = argmax_first(v)
    i2 = argmax_first([jnp.where(i1 == j, -jnp.inf, v[j]) for j in range(EXPERTS_PER_GROUP)])
    s1 = functools.reduce(lambda p, q: p + q, [jnp.where(i1 == j, sc[j], 0.0) for j in range(EXPERTS_PER_GROUP)])
    s2 = functools.reduce(lambda p, q: p + q, [jnp.where(i2 == j, sc[j], 0.0) for j in range(EXPERTS_PER_GROUP)])
    inv = 1.0 / (s1 + s2)
    gate = [jnp.where(i1 == j, s1 * inv, 0.0) + jnp.where(i2 == j, s2 * inv, 0.0) for j in range(EXPERTS_PER_GROUP)]

    lo, hi = jnp.minimum(i1, i2), jnp.maximum(i1, i2)
    pair = jnp.where(lo == 0, hi - 1, jnp.where(lo == 1, hi + 1, N_PAIRS - 1))
    bucket = gb * N_PAIRS + pair
    g_lo = functools.reduce(lambda p, q: p + q, [jnp.where(lo == j, gate[j], 0.0) for j in range(EXPERTS_PER_GROUP)])
    g_hi = functools.reduce(lambda p, q: p + q, [jnp.where(hi == j, gate[j], 0.0) for j in range(EXPERTS_PER_GROUP)])

    memb = jnp.concatenate([(bucket == k).astype(f32) for k in range(N_BUCKETS)]
                           + [jnp.zeros((BUCKET_ROWS - N_BUCKETS, tm), f32)], axis=0)
    before = _dot(memb.astype(bf16), tri_ref[...])
    rank = jnp.sum(memb * (before + carry_ref[:, 0:1]), axis=0, keepdims=True)
    carry_ref[...] = carry_ref[...] + jnp.sum(memb, axis=1, keepdims=True)
    cnt_ref[...] = carry_ref[...]
    bucket_ref[0, 0] = bucket
    rank_ref[0, 0] = rank.astype(i32)

    tail = jnp.concatenate([g_lo, g_hi, jnp.zeros((MOE_TAIL - 2, tm), f32)], axis=0)
    rows_ref[0, :, D_MODEL:MOE_ROW] = tail.T


def _outproj_router(hy, cf, da, w_out_bf, x, gate2, norm2_g, shift2, scale2, wrt_hi, wrt_lo, b_router_col):
    b, l, d = x.shape
    tm = min(512, l)
    row = lambda bi, i: (bi, i, 0)
    per_b = lambda bi, i: (bi, 0, 0)
    const = lambda bi, i: (0, 0)
    mod = pl.BlockSpec((1, 1, d), per_b)
    tri = (jnp.arange(tm, dtype=i32)[:, None] < jnp.arange(tm, dtype=i32)[None, :]).astype(bf16)
    idx_spec = pl.BlockSpec((1, 1, 1, tm), lambda bi, i: (bi, i, 0, 0))
    idx_shape = jax.ShapeDtypeStruct((b, l // tm, 1, tm), i32)
    xo, rows, bucket, rank, cnt = pl.pallas_call(
        _outproj_body,
        grid=(b, l // tm),
        in_specs=[pl.BlockSpec((1, tm, HY_W), row), pl.BlockSpec((1, tm, CF_W), row), pl.BlockSpec((1, tm, DA_W), row),
                  pl.BlockSpec((MIX_W, d), const), pl.BlockSpec((1, tm, d), row), mod,
                  pl.BlockSpec((1, d), const), mod, mod,
                  pl.BlockSpec((N_EXPERTS, d), const), pl.BlockSpec((N_EXPERTS, d), const),
                  pl.BlockSpec((N_EXPERTS, 1), const), pl.BlockSpec((tm, tm), const)],
        out_specs=[pl.BlockSpec((1, tm, d), row), pl.BlockSpec((1, tm, MOE_ROW), row), idx_spec, idx_spec,
                   pl.BlockSpec((BUCKET_ROWS, DA_VD), const)],
        out_shape=[jax.ShapeDtypeStruct((b, l, d), f32), jax.ShapeDtypeStruct((b, l, MOE_ROW), f32), idx_shape, idx_shape,
                   jax.ShapeDtypeStruct((BUCKET_ROWS, DA_VD), f32)],
        scratch_shapes=[pltpu.VMEM((BUCKET_ROWS, DA_VD), f32)],
        compiler_params=_params("arbitrary", "arbitrary"),
        name="outproj_router",
    )(hy, cf, da, w_out_bf, x, gate2, norm2_g, shift2, scale2, wrt_hi, wrt_lo, b_router_col, tri)
    nt = b * l // tm
    return xo, rows.reshape(b * l, MOE_ROW), bucket.reshape(nt, 1, tm), rank.reshape(nt, 1, tm), cnt


def _row_copy(src_ref, src_row, dst_ref, dst_row, sem):
    return pltpu.make_async_copy(src_ref.at[pl.ds(src_row, 1)], dst_ref.at[pl.ds(dst_row, 1)], sem)


def _dispatch_body(slot_ref, rows_ref, xs_ref, sem):
    tm = rows_ref.shape[0]

    def issue(r, carry):
        _row_copy(rows_ref, r, xs_ref, slot_ref[0, 0, r], sem).start()
        return carry

    lax.fori_loop(0, tm, issue, 0, unroll=8)
    pltpu.make_async_copy(rows_ref, xs_ref.at[pl.ds(0, tm)], sem).wait()


def _dispatch(rows, slot, n_rows):
    n_tok = rows.shape[0]
    tm = slot.shape[2]
    return pl.pallas_call(
        _dispatch_body,
        grid=(n_tok // tm,),
        in_specs=[pl.BlockSpec((1, 1, tm), lambda i: (i, 0, 0), memory_space=pltpu.SMEM),
                  pl.BlockSpec((tm, MOE_ROW), lambda i: (i, 0))],
        out_specs=pl.BlockSpec(memory_space=pl.ANY),
        out_shape=jax.ShapeDtypeStruct((n_rows, MOE_ROW), f32),
        scratch_shapes=[pltpu.SemaphoreType.DMA],
        compiler_params=_params("arbitrary"),
        name="moe_dispatch",
    )(slot, rows)


def _ffn_body(oblk_ref, e0_ref, e1_ref, nrow_ref, xs_ref, w1_ref, w3_ref, w2_ref, ys_ref, acc_ref):
    i = pl.program_id(0)
    j = pl.program_id(1)
    nrow = nrow_ref[i]

    @pl.when(nrow > 0)
    def _():
        tmf = xs_ref.shape[0]
        valid = lax.broadcasted_iota(i32, (tmf, 1), 0) < nrow
        h = jnp.where(valid, xs_ref[:, 0:D_MODEL], 0.0).astype(bf16)
        tail = xs_ref[:, D_MODEL:MOE_ROW]
        lane = lax.broadcasted_iota(i32, tail.shape, 1)
        gate = jnp.sum(jnp.where(valid & (lane == j), tail, 0.0), axis=-1, keepdims=True)
        a = _dot(h, w1_ref[0])
        a = a * jax.nn.sigmoid(a) * _dot(h, w3_ref[0])
        y = gate * _dot(a.astype(bf16), w2_ref[0])

        @pl.when(j == 0)
        def _():
            acc_ref[...] = y

        @pl.when(j > 0)
        def _():
            acc_ref[...] += y

        @pl.when(j == TOP_K - 1)
        def _():
            ys_ref[...] = acc_ref[...]


def _ffn(xs, oblk, e0, e1, nrow, w1_bf, w3_bf, w2_bf, tmf):
    n_rows = xs.shape[0]
    d = D_MODEL
    wmap = lambda i, j, oblk, e0, e1, nrow: (jnp.where(j == 0, e0[i], e1[i]), 0, 0)
    grid_spec = pltpu.PrefetchScalarGridSpec(
        num_scalar_prefetch=4,
        grid=(n_rows // tmf, TOP_K),
        in_specs=[pl.BlockSpec((tmf, MOE_ROW), lambda i, j, oblk, e0, e1, nrow: (i, 0)),
                  pl.BlockSpec((1, d, D_EXPERT), wmap), pl.BlockSpec((1, d, D_EXPERT), wmap),
                  pl.BlockSpec((1, D_EXPERT, d), wmap)],
        out_specs=pl.BlockSpec((tmf, d), lambda i, j, oblk, e0, e1, nrow: (oblk[i], 0)),
        scratch_shapes=[pltpu.VMEM((tmf, d), f32)])
    return pl.pallas_call(
        _ffn_body,
        grid_spec=grid_spec,
        out_shape=jax.ShapeDtypeStruct((n_rows + tmf, d), f32),
        compiler_params=_params("arbitrary", "arbitrary"),
        name="moe_ffn",
    )(oblk, e0, e1, nrow, xs, w1_bf, w3_bf, w2_bf)


def _combine_body(slot_ref, ys_ref, x_ref, g5_ref, o_ref, buf_ref, sem):
    tm = buf_ref.shape[0]

    def issue(r, carry):
        _row_copy(ys_ref, slot_ref[0, 0, r], buf_ref, r, sem).start()
        return carry

    lax.fori_loop(0, tm, issue, 0, unroll=8)
    pltpu.make_async_copy(ys_ref.at[pl.ds(0, tm)], buf_ref, sem).wait()
    o_ref[0] = x_ref[0] + g5_ref[0] * buf_ref[...]


def _combine(ys, slot, x, gate5):
    b, l, d = x.shape
    tm = slot.shape[2]
    nt = l // tm
    return pl.pallas_call(
        _combine_body,
        grid=(b, nt),
        in_specs=[pl.BlockSpec((1, 1, tm), lambda bi, i: (bi * nt + i, 0, 0), memory_space=pltpu.SMEM),
                  pl.BlockSpec(memory_space=pl.ANY),
                  pl.BlockSpec((1, tm, d), lambda bi, i: (bi, i, 0)),
                  pl.BlockSpec((1, 1, d), lambda bi, i: (bi, 0, 0))],
        out_specs=pl.BlockSpec((1, tm, d), lambda bi, i: (bi, i, 0)),
        out_shape=jax.ShapeDtypeStruct((b, l, d), f32),
        scratch_shapes=[pltpu.VMEM((tm, d), f32), pltpu.SemaphoreType.DMA],
        compiler_params=_params("arbitrary", "arbitrary"),
        name="moe_combine",
    )(slot, ys, x, gate5)


def _tile_tables(cnt, tmf, n_tiles):
    counts = cnt[:N_BUCKETS, 0].astype(i32)
    tiles = (counts + tmf - 1) // tmf
    ends = jnp.cumsum(tiles)
    starts = ends - tiles
    i = jnp.arange(n_tiles, dtype=i32)
    bkt = jnp.minimum(jnp.sum((i[:, None] >= ends[None, :]).astype(i32), axis=1), N_BUCKETS - 1)
    k = i - starts[bkt]
    used = i < ends[-1]
    oblk = jnp.where(used, i, n_tiles)
    nrow = jnp.where(used, jnp.clip(counts[bkt] - k * tmf, 0, tmf), 0)
    first = jnp.asarray(PAIR_LO, i32)[bkt % N_PAIRS] + EXPERTS_PER_GROUP * (bkt // N_PAIRS)
    second = jnp.asarray(PAIR_HI, i32)[bkt % N_PAIRS] + EXPERTS_PER_GROUP * (bkt // N_PAIRS)
    return starts * tmf, oblk.astype(i32), first.astype(i32), second.astype(i32), nrow.astype(i32)


def _moe(rows, bucket, rank, cnt, x, gate5, w1_bf, w3_bf, w2_bf):
    n_tok = rows.shape[0]
    tmf = min(MOE_TMF, n_tok)
    n_tiles = n_tok // tmf + N_BUCKETS
    first_row, oblk, e0, e1, nrow = _tile_tables(cnt, tmf, n_tiles)
    slot = first_row[bucket] + rank
    xs = _dispatch(rows, slot, n_tiles * tmf)
    ys = _ffn(xs, oblk, e0, e1, nrow, w1_bf, w3_bf, w2_bf, tmf)
    return _combine(ys, slot, x, gate5)


def _rope_tables(n_lat):
    rows = n_lat // GRID_W
    row = jnp.repeat(jnp.arange(rows, dtype=f32), GRID_W)
    col = jnp.tile(jnp.arange(GRID_W, dtype=f32), rows)
    inv = ROPE_THETA ** (-jnp.arange(ROPE_AXIS_FREQS, dtype=f32) / ROPE_AXIS_FREQS)
    ang = jnp.concatenate([row[:, None] * inv, col[:, None] * inv], axis=-1)
    cos, sin = jnp.cos(ang), jnp.sin(ang)
    reps = DA_QK // DA_HD
    return jnp.tile(jnp.concatenate([cos, cos], -1), (1, reps)), jnp.tile(jnp.concatenate([-sin, sin], -1), (1, reps))


def _dft_tables(l):
    n = 2 * l
    f = jnp.arange(l, dtype=i32)[:, None]
    t = jnp.arange(l, dtype=i32)[None, :]
    ang = ((f * t) % n).astype(f32) * (2.0 * math.pi / n)
    cm = jnp.cos(ang)
    sm = jnp.where(f == 0, jnp.where(t % 2 == 0, 1.0, -1.0), jnp.sin(ang))
    return cm.astype(bf16), sm.astype(bf16), sm.T.astype(bf16)


def _hy_features(l):
    t_idx = jnp.arange(l, dtype=f32)[:, None]
    t01 = t_idx / max(l - 1, 1)
    bands = jnp.linspace(1e-4, HY_BANDS - 1, HY_BANDS, dtype=f32)
    ang = 2.0 * math.pi * bands * t_idx / l
    feats = jnp.concatenate([t01, jnp.cos(ang), -jnp.sin(ang)], axis=-1)
    return jnp.pad(feats, ((0, 0), (0, HY_EMB_PAD - HY_EMB)))


def _hyena(u_hy, dft, feats, deltas, conv_w, conv_b, w1p, b1, w2, b2, w3, freq, bias):
    cm, sm, smt = dft
    x0, z = _hyprep(u_hy, conv_w, conv_b)
    hs, hd = _hy_filter(feats, w1p, b1, w2, b2, w3, freq, deltas)
    a, bco, dco = _hy_spectrum(cm, sm, hs, hd)
    return _fftconv(z, x0, cm, sm, smt, a, bco, dco, bias)


def kernel(x, c, ctx, c_ctx, w_ada, b_ada, norm1_g, norm2_g, w_in, hy_conv_w, hy_conv_b, hy_w1, hy_b1, hy_w2, hy_b2,
           hy_w3, hy_freq, hy_bias, cf_dw_w, cf_dw_b, cf_ln_g, cf_ln_b, da_qn_g, da_kn_g, da_lam, da_subln_g, w_out,
           w_router, b_router, moe_w1, moe_w3, moe_w2):
    depth = w_ada.shape[0]
    bsz, n_lat, d = x.shape
    n_ctx = ctx.shape[1]

    rope = _rope_tables(n_lat)
    dft_lat, dft_ctx = _dft_tables(n_lat), _dft_tables(n_ctx)
    feats_lat, feats_ctx = _hy_features(n_lat), _hy_features(n_ctx)
    deltas = jnp.abs(jnp.linspace(HY_MIN_DECAY, HY_MAX_DECAY, HY_W, dtype=f32)).reshape(1, HY_W)
    seg = jnp.arange(DA_QK, dtype=i32) // DA_HD
    bd = (seg[:, None] == seg[None, :]).astype(bf16)

    rows = jnp.concatenate([c, c_ctx[None], jnp.zeros((7, d), f32)], axis=0)
    mod_all = _ada(rows, w_ada, b_ada)

    wr_hi = w_router.T.astype(bf16)
    wr_lo = (w_router.T - wr_hi.astype(f32)).astype(bf16)
    br = b_router.reshape(N_EXPERTS, 1)
    hy_w1p = jnp.pad(hy_w1, ((0, 0), (0, HY_EMB_PAD - HY_EMB), (0, 0)))

    x_lat, x_ctx = x, ctx
    for l in range(depth):
        last = l == depth - 1
        lam_init = 0.8 - 0.6 * math.exp(-0.3 * l)
        mod = mod_all[l, :bsz].reshape(bsz, 6, 1, d)
        mod_c = jnp.broadcast_to(mod_all[l, bsz].reshape(1, 6, 1, d), (bsz, 6, 1, d))
        w_in_bf = w_in[l].astype(bf16)
        w_out_bf = w_out[l].astype(bf16)
        w1_bf, w3_bf, w2_bf = moe_w1[l].astype(bf16), moe_w3[l].astype(bf16), moe_w2[l].astype(bf16)
        g1 = norm1_g[l].reshape(1, d)
        g2 = norm2_g[l].reshape(1, d)
        qg = jnp.tile(da_qn_g[l], DA_QK // DA_HD).reshape(1, DA_QK)
        kg = jnp.tile(da_kn_g[l], DA_QK // DA_HD).reshape(1, DA_QK)
        hy_p = (hy_conv_w[l], hy_conv_b[l], hy_w1p[l], hy_b1[l], hy_w2[l], hy_b2[l], hy_w3[l], hy_freq[l], hy_bias[l])
        cf_p = (cf_dw_w[l], cf_dw_b[l], cf_ln_g[l], cf_ln_b[l])

        uhy, ucf, q_lat, k_lat, v_lat = _inproj(x_lat, mod[:, 0], mod[:, 1], g1, w_in_bf, qg, kg, bd, rope)
        uhy_c, ucf_c, q_ctx, k_ctx, v_ctx = _inproj(x_ctx, mod_c[:, 0], mod_c[:, 1], g1, w_in_bf, qg, kg, bd, None)
        hy_lat = _hyena(uhy, dft_lat, feats_lat, deltas, *hy_p)
        cf_lat = _conformer(ucf, *cf_p)
        da_lat = _diff_attn(q_lat, (k_ctx, v_ctx), (k_lat, v_lat), da_lam[l], da_subln_g[l], lam_init)
        x_lat, *routed = _outproj_router(hy_lat, cf_lat, da_lat, w_out_bf, x_lat, mod[:, 2], g2, mod[:, 3], mod[:, 4],
                                         wr_hi, wr_lo, br)
        x_lat = _moe(*routed, x_lat, mod[:, 5], w1_bf, w3_bf, w2_bf)
        if not last:
            hy_c = _hyena(uhy_c, dft_ctx, feats_ctx, deltas, *hy_p)
            cf_c = _conformer(ucf_c, *cf_p)
            da_c = _diff_attn(q_ctx, (k_ctx, v_ctx), None, da_lam[l], da_subln_g[l], lam_init)
            x_ctx, *routed_c = _outproj_router(hy_c, cf_c, da_c, w_out_bf, x_ctx, mod_c[:, 2], g2, mod_c[:, 3],
                                               mod_c[:, 4], wr_hi, wr_lo, br)
            x_ctx = _moe(*routed_c, x_ctx, mod_c[:, 5], w1_bf, w3_bf, w2_bf)
    return x_lat
```

```python
import functools
import math

import jax
import jax.numpy as jnp
from jax import lax
from jax.experimental import pallas as pl
from jax.experimental.pallas import tpu as pltpu

f32 = jnp.float32
bf16 = jnp.bfloat16
i32 = jnp.int32

D_MODEL = 1024
GRID_W = 64
HY_W = 256
CF_W = 256
DA_HEADS = 4
DA_HD = 64
DA_VD = 2 * DA_HD
DA_W = DA_HEADS * DA_VD
MIX_W = HY_W + CF_W + DA_W
HY_IN = 3 * HY_W
CF_IN = 2 * CF_W
DA_QK = DA_HEADS * 2 * DA_HD
OFF_CF = HY_IN
OFF_Q = OFF_CF + CF_IN
OFF_K = OFF_Q + DA_QK
OFF_V = OFF_K + DA_QK
IN_W = OFF_V + DA_W
CF_CONV = 31
HY_BANDS = 16
HY_EMB = 1 + 2 * HY_BANDS
HY_EMB_PAD = 64
HY_FF = 64
HY_MIN_DECAY = math.log(1e-2) / 1.5
HY_MAX_DECAY = math.log(1e-2) / 0.3
ROPE_THETA = 10000.0
ROPE_AXIS_FREQS = DA_HD // 4
N_EXPERTS = 16
N_GROUPS = 4
EXPERTS_PER_GROUP = N_EXPERTS // N_GROUPS
D_EXPERT = 512
EPS = 1e-6

VMEM_LIMIT_BYTES = 56 * 1024 * 1024
CF_HALO = 16
BF16_ROWS = 16
SUBLANES = 8
ATT_TQ = 256
ATT_NSUB = 8
ATT_CK = 512
LOG2E = 1.4426950408889634
MOE_TAIL = 128
MOE_ROW = D_MODEL + MOE_TAIL
MOE_TMF = 512
TOP_K = 2
PAIR_LO = (0, 0, 0, 1, 1, 2)
PAIR_HI = (1, 2, 3, 2, 3, 3)
N_PAIRS = len(PAIR_LO)
N_BUCKETS = N_GROUPS * N_PAIRS
BUCKET_ROWS = 32


def _params(*sem):
    return pltpu.CompilerParams(dimension_semantics=sem, vmem_limit_bytes=VMEM_LIMIT_BYTES)


def _dot(a, b):
    return jnp.dot(a, b, preferred_element_type=f32)


def _dot_nt(a, b):
    return lax.dot_general(a, b, (((1,), (1,)), ((), ())), preferred_element_type=f32)


def _split_bf16(v):
    hi = v.astype(bf16)
    lo = (v - hi.astype(f32)).astype(bf16)
    return hi, lo


def _ada_body(c_ref, w_ref, b_ref, o_ref):
    c = c_ref[...]
    s = c * jax.nn.sigmoid(c)
    o_ref[0] = _dot(s, w_ref[0]) + b_ref[0]


def _ada(cc, w_ada, b_ada):
    depth, d, n = w_ada.shape
    r = cc.shape[0]
    tn = 1536
    return pl.pallas_call(
        _ada_body,
        grid=(depth, n // tn),
        in_specs=[pl.BlockSpec((r, d), lambda l, j: (0, 0)),
                  pl.BlockSpec((1, d, tn), lambda l, j: (l, 0, j)),
                  pl.BlockSpec((1, 1, tn), lambda l, j: (l, 0, j))],
        out_specs=pl.BlockSpec((1, r, tn), lambda l, j: (l, 0, j)),
        out_shape=jax.ShapeDtypeStruct((depth, r, n), f32),
        compiler_params=_params("parallel", "parallel"),
        name="ada_mod",
    )(cc, w_ada, b_ada.reshape(depth, 1, n))


def _inproj_body(rope, x_ref, sh_ref, sc_ref, g_ref, w_ref, qg_ref, kg_ref, bd_ref, *rest):
    if rope:
        cos_ref, sin_ref, hy_ref, cf_ref, q_ref, k_ref, v_ref = rest
    else:
        hy_ref, cf_ref, q_ref, k_ref, v_ref = rest
    x = x_ref[0]
    ms = jnp.mean(x * x, axis=-1, keepdims=True)
    h = x * lax.rsqrt(ms + EPS) * g_ref[...]
    h = h * (1.0 + sc_ref[0]) + sh_ref[0]
    hb = h.astype(bf16)

    def proj(lo, hi):
        return _dot(hb, w_ref[:, lo:hi])

    hy_ref[0] = proj(0, OFF_CF).astype(bf16)
    cf_ref[0] = proj(OFF_CF, OFF_Q).astype(bf16)
    v_ref[0] = proj(OFF_V, IN_W).astype(bf16)

    lane = lax.broadcasted_iota(i32, (1, DA_QK), 1)
    first_half = (lane % DA_HD) < (DA_HD // 2)

    def qk_heads(lo, hi, gain_ref, out_ref, scale):
        t = proj(lo, hi)
        ss = _dot((t * t).astype(bf16), bd_ref[...])
        tn = t * lax.rsqrt(ss * (1.0 / DA_HD) + EPS) * gain_ref[...]
        if rope:
            half = DA_HD // 2
            partner = jnp.where(first_half, pltpu.roll(tn, DA_QK - half, 1), pltpu.roll(tn, half, 1))
            tn = tn * cos_ref[...] + partner * sin_ref[...]
        out_ref[0] = (tn * scale).astype(bf16)

    qk_heads(OFF_Q, OFF_K, qg_ref, q_ref, LOG2E * DA_HD ** -0.5)
    qk_heads(OFF_K, OFF_V, kg_ref, k_ref, 1.0)


def _inproj(x, shift, scale, g, w_bf, qg, kg, bd, rope_tabs):
    b, l, d = x.shape
    tm = min(512, l)
    rope = rope_tabs is not None
    row = lambda bi, i: (bi, i, 0)
    per_b = lambda bi, i: (bi, 0, 0)
    const = lambda bi, i: (0, 0)
    in_specs = [pl.BlockSpec((1, tm, d), row),
                pl.BlockSpec((1, 1, d), per_b),
                pl.BlockSpec((1, 1, d), per_b),
                pl.BlockSpec((1, d), const),
                pl.BlockSpec((d, IN_W), const),
                pl.BlockSpec((1, DA_QK), const),
                pl.BlockSpec((1, DA_QK), const),
                pl.BlockSpec((DA_QK, DA_QK), const)]
    args = [x, shift, scale, g, w_bf, qg, kg, bd]
    if rope:
        in_specs += [pl.BlockSpec((tm, DA_QK), lambda bi, i: (i, 0))] * 2
        args += list(rope_tabs)
    widths = (HY_IN, CF_IN, DA_QK, DA_QK, DA_W)
    return pl.pallas_call(
        functools.partial(_inproj_body, rope),
        grid=(b, l // tm),
        in_specs=in_specs,
        out_specs=[pl.BlockSpec((1, tm, w), row) for w in widths],
        out_shape=[jax.ShapeDtypeStruct((b, l, w), bf16) for w in widths],
        compiler_params=_params("parallel", "parallel"),
        name="inproj_rope" if rope else "inproj_ctx",
    )(*args)


def _hyprep_body(u_ref, w_ref, b_ref, x0_ref, z_ref):
    l = u_ref.shape[1]
    t = min(128, l)
    n = l // t
    row = lax.broadcasted_iota(i32, (t, 1), 0)

    def chunk(i, carry):
        t0 = pl.multiple_of(i * t, t)
        tp = pl.multiple_of(jnp.maximum(t0 - BF16_ROWS, 0), BF16_ROWS)
        tx = pl.multiple_of(jnp.minimum(t0 + t, l - BF16_ROWS), BF16_ROWS)
        outs = []
        for gi in range(3):
            ls = slice(gi * HY_W, (gi + 1) * HY_W)
            a = u_ref[0, pl.ds(t0, t), ls].astype(f32)
            prev = u_ref[0, pl.ds(tp, BF16_ROWS), ls].astype(f32)[BF16_ROWS - 1:BF16_ROWS]
            prev = jnp.where(i > 0, prev, 0.0)
            nxt = u_ref[0, pl.ds(tx, BF16_ROWS), ls].astype(f32)[0:1]
            nxt = jnp.where(i < n - 1, nxt, 0.0)
            um = jnp.where(row == 0, prev, pltpu.roll(a, 1, 0))
            up = jnp.where(row == t - 1, nxt, pltpu.roll(a, t - 1, 0))
            w = w_ref[:, ls]
            outs.append(w[0:1] * um + w[1:2] * a + w[2:3] * up + b_ref[:, ls])
        x0, x1, v = outs
        x0_ref[0, pl.ds(t0, t), :] = x0.astype(bf16)
        z_ref[0, pl.ds(t0, t), :] = (x1 * v).astype(bf16)
        return carry

    lax.fori_loop(0, n, chunk, 0)


def _hyprep(u_hy, conv_w, conv_b):
    b, l, _ = u_hy.shape
    per_b = lambda bi: (bi, 0, 0)
    const = lambda bi: (0, 0)
    return pl.pallas_call(
        _hyprep_body,
        grid=(b,),
        in_specs=[pl.BlockSpec((1, l, HY_IN), per_b),
                  pl.BlockSpec((3, HY_IN), const),
                  pl.BlockSpec((1, HY_IN), const)],
        out_specs=[pl.BlockSpec((1, l, HY_W), per_b)] * 2,
        out_shape=[jax.ShapeDtypeStruct((b, l, HY_W), bf16)] * 2,
        compiler_params=_params("parallel"),
        name="hyena_prep",
    )(u_hy, conv_w, conv_b.reshape(1, HY_IN))


def _filter_body(feat_ref, w1_ref, b1_ref, w2_ref, b2_ref, w3_ref, fr_ref, dl_ref, hs_ref, hd_ref):
    tl = feat_ref.shape[0]
    feats = feat_ref[...]
    fr = fr_ref[...]
    hid = jnp.sin(fr * (_dot(feats, w1_ref[...]) + b1_ref[...]))
    hid = jnp.sin(fr * (_dot(hid, w2_ref[...]) + b2_ref[...]))
    h = _dot(hid, w3_ref[...])
    win = jnp.exp(-feats[:, 0:1] * dl_ref[...])
    fwd = h[:, :HY_W] * win
    bwd = h[:, HY_W:] * win
    row = pl.program_id(0) * tl + lax.broadcasted_iota(i32, (tl, 1), 0)
    bwd = jnp.where(row == 0, 0.0, bwd)
    hs_ref[...] = fwd + bwd
    hd_ref[...] = bwd - fwd


def _hy_filter(feats, w1p, b1, w2, b2, w3, freq, deltas):
    l = feats.shape[0]
    tl = min(512, l)
    const = lambda i: (0, 0)
    return pl.pallas_call(
        _filter_body,
        grid=(l // tl,),
        in_specs=[pl.BlockSpec((tl, HY_EMB_PAD), lambda i: (i, 0)),
                  pl.BlockSpec((HY_EMB_PAD, HY_FF), const),
                  pl.BlockSpec((1, HY_FF), const),
                  pl.BlockSpec((HY_FF, HY_FF), const),
                  pl.BlockSpec((1, HY_FF), const),
                  pl.BlockSpec((HY_FF, 2 * HY_W), const),
                  pl.BlockSpec((1, HY_FF), const),
                  pl.BlockSpec((1, HY_W), const)],
        out_specs=[pl.BlockSpec((tl, HY_W), lambda i: (i, 0))] * 2,
        out_shape=[jax.ShapeDtypeStruct((l, HY_W), f32)] * 2,
        compiler_params=_params("parallel"),
        name="hyena_filter",
    )(feats, w1p, b1.reshape(1, HY_FF), w2, b2.reshape(1, HY_FF), w3, freq.reshape(1, HY_FF), deltas)


def _kf_body(inv_n, c_ref, s_ref, hs_ref, hd_ref, a_ref, b_ref, d_ref):
    tf = c_ref.shape[0]
    l = hs_ref.shape[0]
    hs = hs_ref[...]
    hs_hi, hs_lo = _split_bf16(hs)
    hd_hi, hd_lo = _split_bf16(hd_ref[...])
    kre = _dot(c_ref[...], hs_hi) + _dot(c_ref[...], hs_lo)
    kim = _dot(s_ref[...], hd_hi) + _dot(s_ref[...], hd_lo)
    tpos = lax.broadcasted_iota(i32, (l, 1), 0)
    knyq = jnp.sum(jnp.where(tpos % 2 == 0, hs, -hs), axis=0, keepdims=True)
    row = pl.program_id(0) * tf + lax.broadcasted_iota(i32, (tf, 1), 0)
    dc = row == 0
    a_ref[...] = jnp.where(dc, kre * inv_n, kre * (2.0 * inv_n))
    b_ref[...] = jnp.where(dc, 0.0, kim * (2.0 * inv_n))
    d_ref[...] = jnp.where(dc, knyq * inv_n, kre * (2.0 * inv_n))


def _hy_spectrum(cm, sm, hs, hd):
    l = hs.shape[0]
    tf = min(512, l)
    full = lambda i: (0, 0)
    rows = lambda i: (i, 0)
    return pl.pallas_call(
        functools.partial(_kf_body, 1.0 / (2 * l)),
        grid=(l // tf,),
        in_specs=[pl.BlockSpec((tf, l), rows), pl.BlockSpec((tf, l), rows),
                  pl.BlockSpec((l, HY_W), full), pl.BlockSpec((l, HY_W), full)],
        out_specs=[pl.BlockSpec((tf, HY_W), rows)] * 3,
        out_shape=[jax.ShapeDtypeStruct((l, HY_W), f32)] * 3,
        compiler_params=_params("parallel"),
        name="hyena_spectrum",
    )(cm, sm, hs, hd)


def _fftconv_body(z_ref, x0_ref, cr_ref, sr_ref, cc_ref, sc_ref, a_ref, b_ref, d_ref, bias_ref, o_ref, acc_ref):
    j = pl.program_id(1)
    bco = b_ref[...]
    for bi in range(z_ref.shape[0]):
        z = z_ref[bi]
        zre = _dot(cr_ref[...], z)
        zs = _dot(sr_ref[...], z)
        yre = (a_ref[...] * zre + bco * zs).astype(bf16)
        yim = (d_ref[...] * zs - bco * zre).astype(bf16)
        part = _dot(cc_ref[...], yre) + _dot(sc_ref[...], yim)

        @pl.when(j == 0)
        def _():
            acc_ref[bi] = part

        @pl.when(j > 0)
        def _():
            acc_ref[bi] += part

        @pl.when(j == pl.num_programs(1) - 1)
        def _():
            y = acc_ref[bi] + bias_ref[...] * z.astype(f32)
            o_ref[bi] = (x0_ref[bi].astype(f32) * y).astype(bf16)


def _fftconv(z, x0, cm, sm, smt, a, bco, dco, bias):
    b, l, _ = z.shape
    tf = min(256, l)
    nb = 1
    per_b = lambda bi, j: (bi, 0, 0)
    frow = lambda bi, j: (j, 0)
    fcol = lambda bi, j: (0, j)
    return pl.pallas_call(
        _fftconv_body,
        grid=(b // nb, l // tf),
        in_specs=[pl.BlockSpec((nb, l, HY_W), per_b), pl.BlockSpec((nb, l, HY_W), per_b),
                  pl.BlockSpec((tf, l), frow), pl.BlockSpec((tf, l), frow),
                  pl.BlockSpec((l, tf), fcol), pl.BlockSpec((l, tf), fcol),
                  pl.BlockSpec((tf, HY_W), frow), pl.BlockSpec((tf, HY_W), frow), pl.BlockSpec((tf, HY_W), frow),
                  pl.BlockSpec((1, HY_W), lambda bi, j: (0, 0))],
        out_specs=pl.BlockSpec((nb, l, HY_W), per_b),
        out_shape=jax.ShapeDtypeStruct((b, l, HY_W), bf16),
        scratch_shapes=[pltpu.VMEM((nb, l, HY_W), f32)],
        compiler_params=_params("parallel", "arbitrary"),
        name="hyena_fftconv",
    )(z, x0, cm, sm, cm, smt, a, bco, dco, bias.reshape(1, HY_W))


def _conformer_body(u_ref, up_ref, un_ref, w_ref, b_ref, g_ref, beta_ref, o_ref, ext_ref):
    i = pl.program_id(1)
    tc = u_ref.shape[1]
    t = min(128, tc)

    def glu(ref, rows):
        a = ref[0, rows, 0:CF_W].astype(f32)
        g = ref[0, rows, CF_W:CF_IN].astype(f32)
        return a * jax.nn.sigmoid(g)

    ext_ref[0, 0:CF_HALO, :] = jnp.where(i > 0, glu(up_ref, slice(None)), 0.0)
    ext_ref[0, CF_HALO + tc:CF_HALO + tc + CF_HALO, :] = jnp.where(i < pl.num_programs(1) - 1, glu(un_ref, slice(None)), 0.0)
    for t0 in range(0, tc, t):
        ext_ref[0, CF_HALO + t0:CF_HALO + t0 + t, :] = glu(u_ref, slice(t0, t0 + t))

    n = tc + 2 * CF_HALO - SUBLANES
    for r in range(1, SUBLANES):
        for c0 in range(0, n, t):
            rows = min(t, n - c0)
            ext_ref[r, c0:c0 + rows, :] = ext_ref[0, c0 + r:c0 + r + rows, :]

    for t0 in range(0, tc, t):
        acc = jnp.zeros((t, CF_W), f32) + b_ref[...]
        for j in range(CF_CONV):
            lo = t0 + CF_HALO - CF_CONV // 2 + j
            r = lo % SUBLANES
            acc = acc + w_ref[j:j + 1, :] * ext_ref[r, lo - r:lo - r + t, :]
        mu = jnp.mean(acc, axis=-1, keepdims=True)
        xc = acc - mu
        var = jnp.mean(xc * xc, axis=-1, keepdims=True)
        y = xc * lax.rsqrt(var + EPS) * g_ref[...] + beta_ref[...]
        o_ref[0, t0:t0 + t, :] = (y * jax.nn.sigmoid(y)).astype(bf16)


def _conformer(u_cf, dw_w, dw_b, ln_g, ln_b):
    b, l, _ = u_cf.shape
    tc = min(512, l)
    hb = tc // CF_HALO
    nh = l // CF_HALO
    const = lambda bi, i: (0, 0)
    vec = pl.BlockSpec((1, CF_W), const)
    return pl.pallas_call(
        _conformer_body,
        grid=(b, l // tc),
        in_specs=[pl.BlockSpec((1, tc, CF_IN), lambda bi, i: (bi, i, 0)),
                  pl.BlockSpec((1, CF_HALO, CF_IN), lambda bi, i: (bi, jnp.maximum(i * hb - 1, 0), 0)),
                  pl.BlockSpec((1, CF_HALO, CF_IN), lambda bi, i: (bi, jnp.minimum((i + 1) * hb, nh - 1), 0)),
                  pl.BlockSpec((CF_CONV, CF_W), const), vec, vec, vec],
        out_specs=pl.BlockSpec((1, tc, CF_W), lambda bi, i: (bi, i, 0)),
        out_shape=jax.ShapeDtypeStruct((b, l, CF_W), bf16),
        scratch_shapes=[pltpu.VMEM((SUBLANES, tc + 2 * CF_HALO, CF_W), f32)],
        compiler_params=_params("parallel", "parallel"),
        name="conformer_conv",
    )(u_cf, u_cf, u_cf, dw_w, dw_b.reshape(1, CF_W), ln_g.reshape(1, CF_W), ln_b.reshape(1, CF_W))


def _attn_body(nsub, tq, ck, n_lat, lam_init, lam_ref, g_ref, q_ref, kc_ref, vc_ref, *rest):
    if n_lat:
        kl_ref, vl_ref = rest[:2]
        rest = rest[2:]
    o_ref, s_a, s_b, mpart_ref, mrow_ref, lpart_ref, acc_ref = rest
    s_bufs = (s_a, s_b)
    n_ctx = kc_ref.shape[1]
    lane = lax.broadcasted_iota(i32, (1, DA_VD), 1)
    lp = lam_ref[...]
    lam = (jnp.exp(jnp.sum(lp[0:1] * lp[1:2], axis=-1, keepdims=True))
           - jnp.exp(jnp.sum(lp[2:3] * lp[3:4], axis=-1, keepdims=True)) + lam_init)

    def stage(t):
        do1, do2 = t < nsub, t >= 1
        s_w, s_r = s_bufs[t % 2], s_bufs[(t - 1) % 2]
        if do1:
            q = q_ref[0, t * tq:(t + 1) * tq, :]
            qm = [jnp.where(lane < DA_HD, q, jnp.zeros_like(q)), jnp.where(lane >= DA_HD, q, jnp.zeros_like(q))]
            mpart_ref[...] = jnp.full(mpart_ref.shape, -jnp.inf, f32)
        if do2:
            lpart_ref[...] = jnp.zeros(lpart_ref.shape, f32)
            acc_ref[...] = jnp.zeros(acc_ref.shape, f32)

        def chunk(k, vt, off, width):
            nb = width // DA_VD
            cols = [slice(off + j * DA_VD, off + (j + 1) * DA_VD) for j in range(nb)]
            if do1:
                for m in range(2):
                    s = _dot_nt(qm[m], k)
                    pm = s[:, 0:DA_VD]
                    for j in range(nb):
                        blk = s[:, j * DA_VD:(j + 1) * DA_VD]
                        s_w[m, :, cols[j]] = blk
                        if j:
                            pm = jnp.maximum(pm, blk)
                    mpart_ref[m] = jnp.maximum(mpart_ref[m], pm)
            if do2:
                for m in range(2):
                    mr = mrow_ref[m]
                    ls = lpart_ref[m]
                    parts = []
                    for j in range(nb):
                        e = jnp.exp2(s_r[m, :, cols[j]] - mr)
                        ls = ls + e
                        parts.append(e.astype(bf16))
                    lpart_ref[m] = ls
                    p = jnp.concatenate(parts, axis=1) if nb > 1 else parts[0]
                    acc_ref[m] += _dot_nt(vt, p)

        chunk(kc_ref[0], vc_ref[0], 0, n_ctx)
        for c in range(n_lat // ck if n_lat else 0):
            chunk(kl_ref[0, c * ck:(c + 1) * ck, :], vl_ref[0, :, c * ck:(c + 1) * ck], n_ctx + c * ck, ck)

        if do2:
            l0 = jnp.sum(lpart_ref[0], axis=-1, keepdims=True)
            l1 = jnp.sum(lpart_ref[1], axis=-1, keepdims=True)
            o = acc_ref[0].T * (1.0 / l0) - acc_ref[1].T * (lam / l1)
            ms = jnp.mean(o * o, axis=-1, keepdims=True)
            o_ref[0, (t - 1) * tq:t * tq, :] = (o * lax.rsqrt(ms + EPS) * g_ref[...] * (1.0 - lam_init)).astype(bf16)
        if do1:
            for m in range(2):
                mrow_ref[m] = jnp.broadcast_to(jnp.max(mpart_ref[m], axis=-1, keepdims=True), (tq, DA_VD))

    for t in range(nsub + 1):
        stage(t)


def _diff_attn(q, kv_ctx, kv_lat, lam_p, subln_g, lam_init):
    b, lq, _ = q.shape
    tq = min(ATT_TQ, lq)
    nsub = min(ATT_NSUB, lq // tq)
    rows = nsub * tq
    n_ctx = kv_ctx[0].shape[1]
    n_lat = kv_lat[0].shape[1] if kv_lat is not None else 0
    ck = min(ATT_CK, n_lat) if n_lat else 0
    lk = n_ctx + n_lat
    head = lambda bi, h, i: (bi, 0, h)
    head_t = lambda bi, h, i: (bi, h, 0)
    in_specs = [pl.BlockSpec((4, DA_HD), lambda bi, h, i: (0, 0)),
                pl.BlockSpec((1, DA_VD), lambda bi, h, i: (0, 0)),
                pl.BlockSpec((1, rows, DA_VD), lambda bi, h, i: (bi, i, h)),
                pl.BlockSpec((1, n_ctx, DA_VD), head), pl.BlockSpec((1, DA_VD, n_ctx), head_t)]
    args = [lam_p, subln_g.reshape(1, DA_VD), q, kv_ctx[0], jnp.swapaxes(kv_ctx[1], 1, 2)]
    if n_lat:
        in_specs += [pl.BlockSpec((1, n_lat, DA_VD), head), pl.BlockSpec((1, DA_VD, n_lat), head_t)]
        args += [kv_lat[0], jnp.swapaxes(kv_lat[1], 1, 2)]
    big = pltpu.VMEM((2, tq, lk), f32)
    pair = pltpu.VMEM((2, tq, DA_VD), f32)
    return pl.pallas_call(
        functools.partial(_attn_body, nsub, tq, ck, n_lat, lam_init),
        grid=(b, DA_HEADS, lq // rows),
        in_specs=in_specs,
        out_specs=pl.BlockSpec((1, rows, DA_VD), lambda bi, h, i: (bi, i, h)),
        out_shape=jax.ShapeDtypeStruct((b, lq, DA_W), bf16),
        scratch_shapes=[big, big, pair, pair, pair, pltpu.VMEM((2, DA_VD, tq), f32)],
        compiler_params=_params("parallel", "parallel", "parallel"),
        name="diff_attn_lat" if n_lat else "diff_attn_ctx",
    )(*args)


def _outproj_body(hy_ref, cf_ref, da_ref, w_ref, x_ref, g2_ref, gn_ref, sh_ref, sc_ref, wr_hi_ref, wr_lo_ref,
                  br_ref, tri_ref, xo_ref, rows_ref, bucket_ref, rank_ref, cnt_ref, carry_ref):
    first = (pl.program_id(0) == 0) & (pl.program_id(1) == 0)

    @pl.when(first)
    def _():
        carry_ref[...] = jnp.zeros(carry_ref.shape, f32)

    y = (_dot(hy_ref[0], w_ref[0:HY_W, :]) + _dot(cf_ref[0], w_ref[HY_W:HY_W + CF_W, :])
         + _dot(da_ref[0], w_ref[HY_W + CF_W:MIX_W, :]))
    x = x_ref[0] + g2_ref[0] * y
    xo_ref[0] = x
    ms = jnp.mean(x * x, axis=-1, keepdims=True)
    h = x * lax.rsqrt(ms + EPS) * gn_ref[...]
    h = h * (1.0 + sc_ref[0]) + sh_ref[0]
    rows_ref[0, :, 0:D_MODEL] = h
    tm = h.shape[0]

    h_hi, h_lo = _split_bf16(h)
    logits = _dot_nt(wr_hi_ref[...], h_hi) + _dot_nt(wr_hi_ref[...], h_lo) + _dot_nt(wr_lo_ref[...], h_hi)
    scores = jax.nn.sigmoid(logits)
    sel = scores + br_ref[...]
    srow = [sel[e:e + 1, :] for e in range(N_EXPERTS)]
    crow = [scores[e:e + 1, :] for e in range(N_EXPERTS)]

    best = None
    for g in range(N_GROUPS):
        a, b, c, d = srow[EXPERTS_PER_GROUP * g:EXPERTS_PER_GROUP * (g + 1)]
        hi1, lo1, hi2, lo2 = jnp.maximum(a, b), jnp.minimum(a, b), jnp.maximum(c, d), jnp.minimum(c, d)
        gs = jnp.maximum(hi1, hi2) + jnp.maximum(jnp.minimum(hi1, hi2), jnp.maximum(lo1, lo2))
        if best is None:
            best, gb = gs, jnp.zeros(gs.shape, i32)
        else:
            better = gs > best
            best = jnp.where(better, gs, best)
            gb = jnp.where(better, g, gb)

    def pick(rows, j):
        out = rows[j]
        for g in range(1, N_GROUPS):
            out = jnp.where(gb == g, rows[EXPERTS_PER_GROUP * g + j], out)
        return out

    v = [pick(srow, j) for j in range(EXPERTS_PER_GROUP)]
    sc = [pick(crow, j) for j in range(EXPERTS_PER_GROUP)]

    def argmax_first(vals):
        idx, m = jnp.zeros(vals[0].shape, i32), vals[0]
        for j in range(1, len(vals)):
            better = vals[j] > m
            idx = jnp.where(better, j, idx)
            m = jnp.where(better, vals[j], m)
        return idx

    i1 = argmax_first(v)
    i2 = argmax_first([jnp.where(i1 == j, -jnp.inf, v[j]) for j in range(EXPERTS_PER_GROUP)])
    s1 = functools.reduce(lambda p, q: p + q, [jnp.where(i1 == j, sc[j], 0.0) for j in range(EXPERTS_PER_GROUP)])
    s2 = functools.reduce(lambda p, q: p + q, [jnp.where(i2 == j, sc[j], 0.0) for j in range(EXPERTS_PER_GROUP)])
    inv = 1.0 / (s1 + s2)
    gate = [jnp.where(i1 == j, s1 * inv, 0.0) + jnp.where(i2 == j, s2 * inv, 0.0) for j in range(EXPERTS_PER_GROUP)]

    lo, hi = jnp.minimum(i1, i2), jnp.maximum(i1, i2)
    pair = jnp.where(lo == 0, hi - 1, jnp.where(lo == 1, hi + 1, N_PAIRS - 1))
    bucket = gb * N_PAIRS + pair
    g_lo = functools.reduce(lambda p, q: p + q, [jnp.where(lo == j, gate[j], 0.0) for j in range(EXPERTS_PER_GROUP)])
    g_hi = functools.reduce(lambda p, q: p + q, [jnp.where(hi == j, gate[j], 0.0) for j in range(EXPERTS_PER_GROUP)])

    memb = jnp.concatenate([(bucket == k).astype(f32) for k in range(N_BUCKETS)]
                           + [jnp.zeros((BUCKET_ROWS - N_BUCKETS, tm), f32)], axis=0)
    before = _dot(memb.astype(bf16), tri_ref[...])
    rank = jnp.sum(memb * (before + carry_ref[:, 0:1]), axis=0, keepdims=True)
    carry_ref[...] = carry_ref[...] + jnp.sum(memb, axis=1, keepdims=True)
    cnt_ref[...] = carry_ref[...]
    bucket_ref[0, 0] = bucket
    rank_ref[0, 0] = rank.astype(i32)

    tail = jnp.concatenate([g_lo, g_hi, jnp.zeros((MOE_TAIL - 2, tm), f32)], axis=0)
    rows_ref[0, :, D_MODEL:MOE_ROW] = tail.T


def _outproj_router(hy, cf, da, w_out_bf, x, gate2, norm2_g, shift2, scale2, wrt_hi, wrt_lo, b_router_col):
    b, l, d = x.shape
    tm = min(512, l)
    row = lambda bi, i: (bi, i, 0)
    per_b = lambda bi, i: (bi, 0, 0)
    const = lambda bi, i: (0, 0)
    mod = pl.BlockSpec((1, 1, d), per_b)
    tri = (jnp.arange(tm, dtype=i32)[:, None] < jnp.arange(tm, dtype=i32)[None, :]).astype(bf16)
    idx_spec = pl.BlockSpec((1, 1, 1, tm), lambda bi, i: (bi, i, 0, 0))
    idx_shape = jax.ShapeDtypeStruct((b, l // tm, 1, tm), i32)
    xo, rows, bucket, rank, cnt = pl.pallas_call(
        _outproj_body,
        grid=(b, l // tm),
        in_specs=[pl.BlockSpec((1, tm, HY_W), row), pl.BlockSpec((1, tm, CF_W), row), pl.BlockSpec((1, tm, DA_W), row),
                  pl.BlockSpec((MIX_W, d), const), pl.BlockSpec((1, tm, d), row), mod,
                  pl.BlockSpec((1, d), const), mod, mod,
                  pl.BlockSpec((N_EXPERTS, d), const), pl.BlockSpec((N_EXPERTS, d), const),
                  pl.BlockSpec((N_EXPERTS, 1), const), pl.BlockSpec((tm, tm), const)],
        out_specs=[pl.BlockSpec((1, tm, d), row), pl.BlockSpec((1, tm, MOE_ROW), row), idx_spec, idx_spec,
                   pl.BlockSpec((BUCKET_ROWS, DA_VD), const)],
        out_shape=[jax.ShapeDtypeStruct((b, l, d), f32), jax.ShapeDtypeStruct((b, l, MOE_ROW), f32), idx_shape, idx_shape,
                   jax.ShapeDtypeStruct((BUCKET_ROWS, DA_VD), f32)],
        scratch_shapes=[pltpu.VMEM((BUCKET_ROWS, DA_VD), f32)],
        compiler_params=_params("arbitrary", "arbitrary"),
        name="outproj_router",
    )(hy, cf, da, w_out_bf, x, gate2, norm2_g, shift2, scale2, wrt_hi, wrt_lo, b_router_col, tri)
    nt = b * l // tm
    return xo, rows.reshape(b * l, MOE_ROW), bucket.reshape(nt, 1, tm), rank.reshape(nt, 1, tm), cnt


def _row_copy(src_ref, src_row, dst_ref, dst_row, sem):
    return pltpu.make_async_copy(src_ref.at[pl.ds(src_row, 1)], dst_ref.at[pl.ds(dst_row, 1)], sem)


def _dispatch_body(slot_ref, rows_ref, xs_ref, sem):
    tm = rows_ref.shape[0]

    def issue(r, carry):
        _row_copy(rows_ref, r, xs_ref, slot_ref[0, 0, r], sem).start()
        return carry

    lax.fori_loop(0, tm, issue, 0, unroll=8)
    pltpu.make_async_copy(rows_ref, xs_ref.at[pl.ds(0, tm)], sem).wait()


def _dispatch(rows, slot, n_rows):
    n_tok = rows.shape[0]
    tm = slot.shape[2]
    return pl.pallas_call(
        _dispatch_body,
        grid=(n_tok // tm,),
        in_specs=[pl.BlockSpec((1, 1, tm), lambda i: (i, 0, 0), memory_space=pltpu.SMEM),
                  pl.BlockSpec((tm, MOE_ROW), lambda i: (i, 0))],
        out_specs=pl.BlockSpec(memory_space=pl.ANY),
        out_shape=jax.ShapeDtypeStruct((n_rows, MOE_ROW), f32),
        scratch_shapes=[pltpu.SemaphoreType.DMA],
        compiler_params=_params("arbitrary"),
        name="moe_dispatch",
    )(slot, rows)


def _ffn_body(oblk_ref, e0_ref, e1_ref, nrow_ref, xs_ref, w1_ref, w3_ref, w2_ref, ys_ref, acc_ref):
    i = pl.program_id(0)
    j = pl.program_id(1)
    nrow = nrow_ref[i]

    @pl.when(nrow > 0)
    def _():
        tmf = xs_ref.shape[0]
        valid = lax.broadcasted_iota(i32, (tmf, 1), 0) < nrow
        h = jnp.where(valid, xs_ref[:, 0:D_MODEL], 0.0).astype(bf16)
        tail = xs_ref[:, D_MODEL:MOE_ROW]
        lane = lax.broadcasted_iota(i32, tail.shape, 1)
        gate = jnp.sum(jnp.where(valid & (lane == j), tail, 0.0), axis=-1, keepdims=True)
        a = _dot(h, w1_ref[0])
        a = a * jax.nn.sigmoid(a) * _dot(h, w3_ref[0])
        y = gate * _dot(a.astype(bf16), w2_ref[0])

        @pl.when(j == 0)
        def _():
            acc_ref[...] = y

        @pl.when(j > 0)
        def _():
            acc_ref[...] += y

        @pl.when(j == TOP_K - 1)
        def _():
            ys_ref[...] = acc_ref[...]


def _ffn(xs, oblk, e0, e1, nrow, w1_bf, w3_bf, w2_bf, tmf):
    n_rows = xs.shape[0]
    d = D_MODEL
    wmap = lambda i, j, oblk, e0, e1, nrow: (jnp.where(j == 0, e0[i], e1[i]), 0, 0)
    grid_spec = pltpu.PrefetchScalarGridSpec(
        num_scalar_prefetch=4,
        grid=(n_rows // tmf, TOP_K),
        in_specs=[pl.BlockSpec((tmf, MOE_ROW), lambda i, j, oblk, e0, e1, nrow: (i, 0)),
                  pl.BlockSpec((1, d, D_EXPERT), wmap), pl.BlockSpec((1, d, D_EXPERT), wmap),
                  pl.BlockSpec((1, D_EXPERT, d), wmap)],
        out_specs=pl.BlockSpec((tmf, d), lambda i, j, oblk, e0, e1, nrow: (oblk[i], 0)),
        scratch_shapes=[pltpu.VMEM((tmf, d), f32)])
    return pl.pallas_call(
        _ffn_body,
        grid_spec=grid_spec,
        out_shape=jax.ShapeDtypeStruct((n_rows + tmf, d), f32),
        compiler_params=_params("arbitrary", "arbitrary"),
        name="moe_ffn",
    )(oblk, e0, e1, nrow, xs, w1_bf, w3_bf, w2_bf)


def _combine_body(slot_ref, ys_ref, x_ref, g5_ref, o_ref, buf_ref, sem):
    tm = buf_ref.shape[0]

    def issue(r, carry):
        _row_copy(ys_ref, slot_ref[0, 0, r], buf_ref, r, sem).start()
        return carry

    lax.fori_loop(0, tm, issue, 0, unroll=8)
    pltpu.make_async_copy(ys_ref.at[pl.ds(0, tm)], buf_ref, sem).wait()
    o_ref[0] = x_ref[0] + g5_ref[0] * buf_ref[...]


def _combine(ys, slot, x, gate5):
    b, l, d = x.shape
    tm = slot.shape[2]
    nt = l // tm
    return pl.pallas_call(
        _combine_body,
        grid=(b, nt),
        in_specs=[pl.BlockSpec((1, 1, tm), lambda bi, i: (bi * nt + i, 0, 0), memory_space=pltpu.SMEM),
                  pl.BlockSpec(memory_space=pl.ANY),
                  pl.BlockSpec((1, tm, d), lambda bi, i: (bi, i, 0)),
                  pl.BlockSpec((1, 1, d), lambda bi, i: (bi, 0, 0))],
        out_specs=pl.BlockSpec((1, tm, d), lambda bi, i: (bi, i, 0)),
        out_shape=jax.ShapeDtypeStruct((b, l, d), f32),
        scratch_shapes=[pltpu.VMEM((tm, d), f32), pltpu.SemaphoreType.DMA],
        compiler_params=_params("arbitrary", "arbitrary"),
        name="moe_combine",
    )(slot, ys, x, gate5)


def _tile_tables(cnt, tmf, n_tiles):
    counts = cnt[:N_BUCKETS, 0].astype(i32)
    tiles = (counts + tmf - 1) // tmf
    ends = jnp.cumsum(tiles)
    starts = ends - tiles
    i = jnp.arange(n_tiles, dtype=i32)
    bkt = jnp.minimum(jnp.sum((i[:, None] >= ends[None, :]).astype(i32), axis=1), N_BUCKETS - 1)
    k = i - starts[bkt]
    used = i < ends[-1]
    oblk = jnp.where(used, i, n_tiles)
    nrow = jnp.where(used, jnp.clip(counts[bkt] - k * tmf, 0, tmf), 0)
    first = jnp.asarray(PAIR_LO, i32)[bkt % N_PAIRS] + EXPERTS_PER_GROUP * (bkt // N_PAIRS)
    second = jnp.asarray(PAIR_HI, i32)[bkt % N_PAIRS] + EXPERTS_PER_GROUP * (bkt // N_PAIRS)
    return starts * tmf, oblk.astype(i32), first.astype(i32), second.astype(i32), nrow.astype(i32)


def _moe(rows, bucket, rank, cnt, x, gate5, w1_bf, w3_bf, w2_bf):
    n_tok = rows.shape[0]
    tmf = min(MOE_TMF, n_tok)
    n_tiles = n_tok // tmf + N_BUCKETS
    first_row, oblk, e0, e1, nrow = _tile_tables(cnt, tmf, n_tiles)
    slot = first_row[bucket] + rank
    xs = _dispatch(rows, slot, n_tiles * tmf)
    ys = _ffn(xs, oblk, e0, e1, nrow, w1_bf, w3_bf, w2_bf, tmf)
    return _combine(ys, slot, x, gate5)


def _rope_tables(n_lat):
    rows = n_lat // GRID_W
    row = jnp.repeat(jnp.arange(rows, dtype=f32), GRID_W)
    col = jnp.tile(jnp.arange(GRID_W, dtype=f32), rows)
    inv = ROPE_THETA ** (-jnp.arange(ROPE_AXIS_FREQS, dtype=f32) / ROPE_AXIS_FREQS)
    ang = jnp.concatenate([row[:, None] * inv, col[:, None] * inv], axis=-1)
    cos, sin = jnp.cos(ang), jnp.sin(ang)
    reps = DA_QK // DA_HD
    return jnp.tile(jnp.concatenate([cos, cos], -1), (1, reps)), jnp.tile(jnp.concatenate([-sin, sin], -1), (1, reps))


def _dft_tables(l):
    n = 2 * l
    f = jnp.arange(l, dtype=i32)[:, None]
    t = jnp.arange(l, dtype=i32)[None, :]
    ang = ((f * t) % n).astype(f32) * (2.0 * math.pi / n)
    cm = jnp.cos(ang)
    sm = jnp.where(f == 0, jnp.where(t % 2 == 0, 1.0, -1.0), jnp.sin(ang))
    return cm.astype(bf16), sm.astype(bf16), sm.T.astype(bf16)


def _hy_features(l):
    t_idx = jnp.arange(l, dtype=f32)[:, None]
    t01 = t_idx / max(l - 1, 1)
    bands = jnp.linspace(1e-4, HY_BANDS - 1, HY_BANDS, dtype=f32)
    ang = 2.0 * math.pi * bands * t_idx / l
    feats = jnp.concatenate([t01, jnp.cos(ang), -jnp.sin(ang)], axis=-1)
    return jnp.pad(feats, ((0, 0), (0, HY_EMB_PAD - HY_EMB)))


def _hyena(u_hy, dft, feats, deltas, conv_w, conv_b, w1p, b1, w2, b2, w3, freq, bias):
    cm, sm, smt = dft
    x0, z = _hyprep(u_hy, conv_w, conv_b)
    hs, hd = _hy_filter(feats, w1p, b1, w2, b2, w3, freq, deltas)
    a, bco, dco = _hy_spectrum(cm, sm, hs, hd)
    return _fftconv(z, x0, cm, sm, smt, a, bco, dco, bias)


def kernel(x, c, ctx, c_ctx, w_ada, b_ada, norm1_g, norm2_g, w_in, hy_conv_w, hy_conv_b, hy_w1, hy_b1, hy_w2, hy_b2,
           hy_w3, hy_freq, hy_bias, cf_dw_w, cf_dw_b, cf_ln_g, cf_ln_b, da_qn_g, da_kn_g, da_lam, da_subln_g, w_out,
           w_router, b_router, moe_w1, moe_w3, moe_w2):
    depth = w_ada.shape[0]
    bsz, n_lat, d = x.shape
    n_ctx = ctx.shape[1]

    rope = _rope_tables(n_lat)
    dft_lat, dft_ctx = _dft_tables(n_lat), _dft_tables(n_ctx)
    feats_lat, feats_ctx = _hy_features(n_lat), _hy_features(n_ctx)
    deltas = jnp.abs(jnp.linspace(HY_MIN_DECAY, HY_MAX_DECAY, HY_W, dtype=f32)).reshape(1, HY_W)
    seg = jnp.arange(DA_QK, dtype=i32) // DA_HD
    bd = (seg[:, None] == seg[None, :]).astype(bf16)

    rows = jnp.concatenate([c, c_ctx[None], jnp.zeros((7, d), f32)], axis=0)
    mod_all = _ada(rows, w_ada, b_ada)

    wr_hi = w_router.T.astype(bf16)
    wr_lo = (w_router.T - wr_hi.astype(f32)).astype(bf16)
    br = b_router.reshape(N_EXPERTS, 1)
    hy_w1p = jnp.pad(hy_w1, ((0, 0), (0, HY_EMB_PAD - HY_EMB), (0, 0)))

    x_lat, x_ctx = x, ctx
    for l in range(depth):
        last = l == depth - 1
        lam_init = 0.8 - 0.6 * math.exp(-0.3 * l)
        mod = mod_all[l, :bsz].reshape(bsz, 6, 1, d)
        mod_c = jnp.broadcast_to(mod_all[l, bsz].reshape(1, 6, 1, d), (bsz, 6, 1, d))
        w_in_bf = w_in[l].astype(bf16)
        w_out_bf = w_out[l].astype(bf16)
        w1_bf, w3_bf, w2_bf = moe_w1[l].astype(bf16), moe_w3[l].astype(bf16), moe_w2[l].astype(bf16)
        g1 = norm1_g[l].reshape(1, d)
        g2 = norm2_g[l].reshape(1, d)
        qg = jnp.tile(da_qn_g[l], DA_QK // DA_HD).reshape(1, DA_QK)
        kg = jnp.tile(da_kn_g[l], DA_QK // DA_HD).reshape(1, DA_QK)
        hy_p = (hy_conv_w[l], hy_conv_b[l], hy_w1p[l], hy_b1[l], hy_w2[l], hy_b2[l], hy_w3[l], hy_freq[l], hy_bias[l])
        cf_p = (cf_dw_w[l], cf_dw_b[l], cf_ln_g[l], cf_ln_b[l])

        uhy, ucf, q_lat, k_lat, v_lat = _inproj(x_lat, mod[:, 0], mod[:, 1], g1, w_in_bf, qg, kg, bd, rope)
        uhy_c, ucf_c, q_ctx, k_ctx, v_ctx = _inproj(x_ctx, mod_c[:, 0], mod_c[:, 1], g1, w_in_bf, qg, kg, bd, None)
        hy_lat = _hyena(uhy, dft_lat, feats_lat, deltas, *hy_p)
        cf_lat = _conformer(ucf, *cf_p)
        da_lat = _diff_attn(q_lat, (k_ctx, v_ctx), (k_lat, v_lat), da_lam[l], da_subln_g[l], lam_init)
        x_lat, *routed = _outproj_router(hy_lat, cf_lat, da_lat, w_out_bf, x_lat, mod[:, 2], g2, mod[:, 3], mod[:, 4],
                                         wr_hi, wr_lo, br)
        x_lat = _moe(*routed, x_lat, mod[:, 5], w1_bf, w3_bf, w2_bf)
        if not last:
            hy_c = _hyena(uhy_c, dft_ctx, feats_ctx, deltas, *hy_p)
            cf_c = _conformer(ucf_c, *cf_p)
            da_c = _diff_attn(q_ctx, (k_ctx, v_ctx), None, da_lam[l], da_subln_g[l], lam_init)
            x_ctx, *routed_c = _outproj_router(hy_c, cf_c, da_c, w_out_bf, x_ctx, mod_c[:, 2], g2, mod_c[:, 3],
                                               mod_c[:, 4], wr_hi, wr_lo, br)
            x_ctx = _moe(*routed_c, x_ctx, mod_c[:, 5], w1_bf, w3_bf, w2_bf)
    return x_lat
```

```python
import functools
import math

import jax
import jax.numpy as jnp
from jax import lax
from jax.experimental import pallas as pl
from jax.experimental.pallas import tpu as pltpu

f32 = jnp.float32
bf16 = jnp.bfloat16
i32 = jnp.int32

D_MODEL = 1024
GRID_W = 64
HY_W = 256
CF_W = 256
DA_HEADS = 4
DA_HD = 64
DA_VD = 2 * DA_HD
DA_W = DA_HEADS * DA_VD
MIX_W = HY_W + CF_W + DA_W
HY_IN = 3 * HY_W
CF_IN = 2 * CF_W
DA_QK = DA_HEADS * 2 * DA_HD
OFF_CF = HY_IN
OFF_Q = OFF_CF + CF_IN
OFF_K = OFF_Q + DA_QK
OFF_V = OFF_K + DA_QK
IN_W = OFF_V + DA_W
CF_CONV = 31
HY_BANDS = 16
HY_EMB = 1 + 2 * HY_BANDS
HY_EMB_PAD = 64
HY_FF = 64
HY_MIN_DECAY = math.log(1e-2) / 1.5
HY_MAX_DECAY = math.log(1e-2) / 0.3
ROPE_THETA = 10000.0
ROPE_AXIS_FREQS = DA_HD // 4
N_EXPERTS = 16
N_GROUPS = 4
EXPERTS_PER_GROUP = N_EXPERTS // N_GROUPS
D_EXPERT = 512
EPS = 1e-6

VMEM_LIMIT_BYTES = 56 * 1024 * 1024
CF_HALO = 16
BF16_ROWS = 16
SUBLANES = 8
ATT_TQ = 256
ATT_NSUB = 8
ATT_CK = 256
LOG2E = 1.4426950408889634
MOE_TAIL = 128
MOE_ROW = D_MODEL + MOE_TAIL
MOE_TMF = 512
TOP_K = 2
PAIR_LO = (0, 0, 0, 1, 1, 2)
PAIR_HI = (1, 2, 3, 2, 3, 3)
N_PAIRS = len(PAIR_LO)
N_BUCKETS = N_GROUPS * N_PAIRS
BUCKET_ROWS = 32


def _params(*sem):
    return pltpu.CompilerParams(dimension_semantics=sem, vmem_limit_bytes=VMEM_LIMIT_BYTES)


def _dot(a, b):
    return jnp.dot(a, b, preferred_element_type=f32)


def _dot_nt(a, b):
    return lax.dot_general(a, b, (((1,), (1,)), ((), ())), preferred_element_type=f32)


def _split_bf16(v):
    hi = v.astype(bf16)
    lo = (v - hi.astype(f32)).astype(bf16)
    return hi, lo


def _ada_body(c_ref, w_ref, b_ref, o_ref):
    c = c_ref[...]
    s = c * jax.nn.sigmoid(c)
    o_ref[0] = _dot(s, w_ref[0]) + b_ref[0]


def _ada(cc, w_ada, b_ada):
    depth, d, n = w_ada.shape
    r = cc.shape[0]
    tn = 1536
    return pl.pallas_call(
        _ada_body,
        grid=(depth, n // tn),
        in_specs=[pl.BlockSpec((r, d), lambda l, j: (0, 0)),
                  pl.BlockSpec((1, d, tn), lambda l, j: (l, 0, j)),
                  pl.BlockSpec((1, 1, tn), lambda l, j: (l, 0, j))],
        out_specs=pl.BlockSpec((1, r, tn), lambda l, j: (l, 0, j)),
        out_shape=jax.ShapeDtypeStruct((depth, r, n), f32),
        compiler_params=_params("parallel", "parallel"),
        name="ada_mod",
    )(cc, w_ada, b_ada.reshape(depth, 1, n))


def _inproj_body(rope, x_ref, sh_ref, sc_ref, g_ref, w_ref, qg_ref, kg_ref, bd_ref, *rest):
    if rope:
        cos_ref, sin_ref, hy_ref, cf_ref, q_ref, k_ref, v_ref = rest
    else:
        hy_ref, cf_ref, q_ref, k_ref, v_ref = rest
    x = x_ref[0]
    ms = jnp.mean(x * x, axis=-1, keepdims=True)
    h = x * lax.rsqrt(ms + EPS) * g_ref[...]
    h = h * (1.0 + sc_ref[0]) + sh_ref[0]
    hb = h.astype(bf16)

    def proj(lo, hi):
        return _dot(hb, w_ref[:, lo:hi])

    hy_ref[0] = proj(0, OFF_CF).astype(bf16)
    cf_ref[0] = proj(OFF_CF, OFF_Q).astype(bf16)
    v_ref[0] = proj(OFF_V, IN_W).astype(bf16)

    lane = lax.broadcasted_iota(i32, (1, DA_QK), 1)
    first_half = (lane % DA_HD) < (DA_HD // 2)

    def qk_heads(lo, hi, gain_ref, out_ref, scale):
        t = proj(lo, hi)
        ss = _dot((t * t).astype(bf16), bd_ref[...])
        tn = t * lax.rsqrt(ss * (1.0 / DA_HD) + EPS) * gain_ref[...]
        if rope:
            half = DA_HD // 2
            partner = jnp.where(first_half, pltpu.roll(tn, DA_QK - half, 1), pltpu.roll(tn, half, 1))
            tn = tn * cos_ref[...] + partner * sin_ref[...]
        out_ref[0] = (tn * scale).astype(bf16)

    qk_heads(OFF_Q, OFF_K, qg_ref, q_ref, LOG2E * DA_HD ** -0.5)
    qk_heads(OFF_K, OFF_V, kg_ref, k_ref, 1.0)


def _inproj(x, shift, scale, g, w_bf, qg, kg, bd, rope_tabs):
    b, l, d = x.shape
    tm = min(512, l)
    rope = rope_tabs is not None
    row = lambda bi, i: (bi, i, 0)
    per_b = lambda bi, i: (bi, 0, 0)
    const = lambda bi, i: (0, 0)
    in_specs = [pl.BlockSpec((1, tm, d), row),
                pl.BlockSpec((1, 1, d), per_b),
                pl.BlockSpec((1, 1, d), per_b),
                pl.BlockSpec((1, d), const),
                pl.BlockSpec((d, IN_W), const),
                pl.BlockSpec((1, DA_QK), const),
                pl.BlockSpec((1, DA_QK), const),
                pl.BlockSpec((DA_QK, DA_QK), const)]
    args = [x, shift, scale, g, w_bf, qg, kg, bd]
    if rope:
        in_specs += [pl.BlockSpec((tm, DA_QK), lambda bi, i: (i, 0))] * 2
        args += list(rope_tabs)
    widths = (HY_IN, CF_IN, DA_QK, DA_QK, DA_W)
    return pl.pallas_call(
        functools.partial(_inproj_body, rope),
        grid=(b, l // tm),
        in_specs=in_specs,
        out_specs=[pl.BlockSpec((1, tm, w), row) for w in widths],
        out_shape=[jax.ShapeDtypeStruct((b, l, w), bf16) for w in widths],
        compiler_params=_params("parallel", "parallel"),
        name="inproj_rope" if rope else "inproj_ctx",
    )(*args)


def _hyprep_body(u_ref, w_ref, b_ref, x0_ref, z_ref):
    l = u_ref.shape[1]
    t = min(128, l)
    n = l // t
    row = lax.broadcasted_iota(i32, (t, 1), 0)

    def chunk(i, carry):
        t0 = pl.multiple_of(i * t, t)
        tp = pl.multiple_of(jnp.maximum(t0 - BF16_ROWS, 0), BF16_ROWS)
        tx = pl.multiple_of(jnp.minimum(t0 + t, l - BF16_ROWS), BF16_ROWS)
        outs = []
        for gi in range(3):
            ls = slice(gi * HY_W, (gi + 1) * HY_W)
            a = u_ref[0, pl.ds(t0, t), ls].astype(f32)
            prev = u_ref[0, pl.ds(tp, BF16_ROWS), ls].astype(f32)[BF16_ROWS - 1:BF16_ROWS]
            prev = jnp.where(i > 0, prev, 0.0)
            nxt = u_ref[0, pl.ds(tx, BF16_ROWS), ls].astype(f32)[0:1]
            nxt = jnp.where(i < n - 1, nxt, 0.0)
            um = jnp.where(row == 0, prev, pltpu.roll(a, 1, 0))
            up = jnp.where(row == t - 1, nxt, pltpu.roll(a, t - 1, 0))
            w = w_ref[:, ls]
            outs.append(w[0:1] * um + w[1:2] * a + w[2:3] * up + b_ref[:, ls])
        x0, x1, v = outs
        x0_ref[0, pl.ds(t0, t), :] = x0.astype(bf16)
        z_ref[0, pl.ds(t0, t), :] = (x1 * v).astype(bf16)
        return carry

    lax.fori_loop(0, n, chunk, 0)


def _hyprep(u_hy, conv_w, conv_b):
    b, l, _ = u_hy.shape
    per_b = lambda bi: (bi, 0, 0)
    const = lambda bi: (0, 0)
    return pl.pallas_call(
        _hyprep_body,
        grid=(b,),
        in_specs=[pl.BlockSpec((1, l, HY_IN), per_b),
                  pl.BlockSpec((3, HY_IN), const),
                  pl.BlockSpec((1, HY_IN), const)],
        out_specs=[pl.BlockSpec((1, l, HY_W), per_b)] * 2,
        out_shape=[jax.ShapeDtypeStruct((b, l, HY_W), bf16)] * 2,
        compiler_params=_params("parallel"),
        name="hyena_prep",
    )(u_hy, conv_w, conv_b.reshape(1, HY_IN))


def _filter_body(feat_ref, w1_ref, b1_ref, w2_ref, b2_ref, w3_ref, fr_ref, dl_ref, hs_ref, hd_ref):
    tl = feat_ref.shape[0]
    feats = feat_ref[...]
    fr = fr_ref[...]
    hid = jnp.sin(fr * (_dot(feats, w1_ref[...]) + b1_ref[...]))
    hid = jnp.sin(fr * (_dot(hid, w2_ref[...]) + b2_ref[...]))
    h = _dot(hid, w3_ref[...])
    win = jnp.exp(-feats[:, 0:1] * dl_ref[...])
    fwd = h[:, :HY_W] * win
    bwd = h[:, HY_W:] * win
    row = pl.program_id(0) * tl + lax.broadcasted_iota(i32, (tl, 1), 0)
    bwd = jnp.where(row == 0, 0.0, bwd)
    hs_ref[...] = fwd + bwd
    hd_ref[...] = bwd - fwd


def _hy_filter(feats, w1p, b1, w2, b2, w3, freq, deltas):
    l = feats.shape[0]
    tl = min(512, l)
    const = lambda i: (0, 0)
    return pl.pallas_call(
        _filter_body,
        grid=(l // tl,),
        in_specs=[pl.BlockSpec((tl, HY_EMB_PAD), lambda i: (i, 0)),
                  pl.BlockSpec((HY_EMB_PAD, HY_FF), const),
                  pl.BlockSpec((1, HY_FF), const),
                  pl.BlockSpec((HY_FF, HY_FF), const),
                  pl.BlockSpec((1, HY_FF), const),
                  pl.BlockSpec((HY_FF, 2 * HY_W), const),
                  pl.BlockSpec((1, HY_FF), const),
                  pl.BlockSpec((1, HY_W), const)],
        out_specs=[pl.BlockSpec((tl, HY_W), lambda i: (i, 0))] * 2,
        out_shape=[jax.ShapeDtypeStruct((l, HY_W), f32)] * 2,
        compiler_params=_params("parallel"),
        name="hyena_filter",
    )(feats, w1p, b1.reshape(1, HY_FF), w2, b2.reshape(1, HY_FF), w3, freq.reshape(1, HY_FF), deltas)


def _kf_body(inv_n, c_ref, s_ref, hs_ref, hd_ref, a_ref, b_ref, d_ref):
    tf = c_ref.shape[0]
    l = hs_ref.shape[0]
    hs = hs_ref[...]
    hs_hi, hs_lo = _split_bf16(hs)
    hd_hi, hd_lo = _split_bf16(hd_ref[...])
    kre = _dot(c_ref[...], hs_hi) + _dot(c_ref[...], hs_lo)
    kim = _dot(s_ref[...], hd_hi) + _dot(s_ref[...], hd_lo)
    tpos = lax.broadcasted_iota(i32, (l, 1), 0)
    knyq = jnp.sum(jnp.where(tpos % 2 == 0, hs, -hs), axis=0, keepdims=True)
    row = pl.program_id(0) * tf + lax.broadcasted_iota(i32, (tf, 1), 0)
    dc = row == 0
    a_ref[...] = jnp.where(dc, kre * inv_n, kre * (2.0 * inv_n))
    b_ref[...] = jnp.where(dc, 0.0, kim * (2.0 * inv_n))
    d_ref[...] = jnp.where(dc, knyq * inv_n, kre * (2.0 * inv_n))


def _hy_spectrum(cm, sm, hs, hd):
    l = hs.shape[0]
    tf = min(512, l)
    full = lambda i: (0, 0)
    rows = lambda i: (i, 0)
    return pl.pallas_call(
        functools.partial(_kf_body, 1.0 / (2 * l)),
        grid=(l // tf,),
        in_specs=[pl.BlockSpec((tf, l), rows), pl.BlockSpec((tf, l), rows),
                  pl.BlockSpec((l, HY_W), full), pl.BlockSpec((l, HY_W), full)],
        out_specs=[pl.BlockSpec((tf, HY_W), rows)] * 3,
        out_shape=[jax.ShapeDtypeStruct((l, HY_W), f32)] * 3,
        compiler_params=_params("parallel"),
        name="hyena_spectrum",
    )(cm, sm, hs, hd)


def _fftconv_body(z_ref, x0_ref, cr_ref, sr_ref, cc_ref, sc_ref, a_ref, b_ref, d_ref, bias_ref, o_ref, acc_ref):
    j = pl.program_id(1)
    bco = b_ref[...]
    for bi in range(z_ref.shape[0]):
        z = z_ref[bi]
        zre = _dot(cr_ref[...], z)
        zs = _dot(sr_ref[...], z)
        yre = (a_ref[...] * zre + bco * zs).astype(bf16)
        yim = (d_ref[...] * zs - bco * zre).astype(bf16)
        part = _dot(cc_ref[...], yre) + _dot(sc_ref[...], yim)

        @pl.when(j == 0)
        def _():
            acc_ref[bi] = part

        @pl.when(j > 0)
        def _():
            acc_ref[bi] += part

        @pl.when(j == pl.num_programs(1) - 1)
        def _():
            y = acc_ref[bi] + bias_ref[...] * z.astype(f32)
            o_ref[bi] = (x0_ref[bi].astype(f32) * y).astype(bf16)


def _fftconv(z, x0, cm, sm, smt, a, bco, dco, bias):
    b, l, _ = z.shape
    tf = min(256, l)
    nb = 1
    per_b = lambda bi, j: (bi, 0, 0)
    frow = lambda bi, j: (j, 0)
    fcol = lambda bi, j: (0, j)
    return pl.pallas_call(
        _fftconv_body,
        grid=(b // nb, l // tf),
        in_specs=[pl.BlockSpec((nb, l, HY_W), per_b), pl.BlockSpec((nb, l, HY_W), per_b),
                  pl.BlockSpec((tf, l), frow), pl.BlockSpec((tf, l), frow),
                  pl.BlockSpec((l, tf), fcol), pl.BlockSpec((l, tf), fcol),
                  pl.BlockSpec((tf, HY_W), frow), pl.BlockSpec((tf, HY_W), frow), pl.BlockSpec((tf, HY_W), frow),
                  pl.BlockSpec((1, HY_W), lambda bi, j: (0, 0))],
        out_specs=pl.BlockSpec((nb, l, HY_W), per_b),
        out_shape=jax.ShapeDtypeStruct((b, l, HY_W), bf16),
        scratch_shapes=[pltpu.VMEM((nb, l, HY_W), f32)],
        compiler_params=_params("parallel", "arbitrary"),
        name="hyena_fftconv",
    )(z, x0, cm, sm, cm, smt, a, bco, dco, bias.reshape(1, HY_W))


def _conformer_body(u_ref, up_ref, un_ref, w_ref, b_ref, g_ref, beta_ref, o_ref, ext_ref):
    i = pl.program_id(1)
    tc = u_ref.shape[1]
    t = min(128, tc)

    def glu(ref, rows):
        a = ref[0, rows, 0:CF_W].astype(f32)
        g = ref[0, rows, CF_W:CF_IN].astype(f32)
        return a * jax.nn.sigmoid(g)

    ext_ref[0, 0:CF_HALO, :] = jnp.where(i > 0, glu(up_ref, slice(None)), 0.0)
    ext_ref[0, CF_HALO + tc:CF_HALO + tc + CF_HALO, :] = jnp.where(i < pl.num_programs(1) - 1, glu(un_ref, slice(None)), 0.0)
    for t0 in range(0, tc, t):
        ext_ref[0, CF_HALO + t0:CF_HALO + t0 + t, :] = glu(u_ref, slice(t0, t0 + t))

    n = tc + 2 * CF_HALO - SUBLANES
    for r in range(1, SUBLANES):
        for c0 in range(0, n, t):
            rows = min(t, n - c0)
            ext_ref[r, c0:c0 + rows, :] = ext_ref[0, c0 + r:c0 + r + rows, :]

    for t0 in range(0, tc, t):
        acc = jnp.zeros((t, CF_W), f32) + b_ref[...]
        for j in range(CF_CONV):
            lo = t0 + CF_HALO - CF_CONV // 2 + j
            r = lo % SUBLANES
            acc = acc + w_ref[j:j + 1, :] * ext_ref[r, lo - r:lo - r + t, :]
        mu = jnp.mean(acc, axis=-1, keepdims=True)
        xc = acc - mu
        var = jnp.mean(xc * xc, axis=-1, keepdims=True)
        y = xc * lax.rsqrt(var + EPS) * g_ref[...] + beta_ref[...]
        o_ref[0, t0:t0 + t, :] = (y * jax.nn.sigmoid(y)).astype(bf16)


def _conformer(u_cf, dw_w, dw_b, ln_g, ln_b):
    b, l, _ = u_cf.shape
    tc = min(512, l)
    hb = tc // CF_HALO
    nh = l // CF_HALO
    const = lambda bi, i: (0, 0)
    vec = pl.BlockSpec((1, CF_W), const)
    return pl.pallas_call(
        _conformer_body,
        grid=(b, l // tc),
        in_specs=[pl.BlockSpec((1, tc, CF_IN), lambda bi, i: (bi, i, 0)),
                  pl.BlockSpec((1, CF_HALO, CF_IN), lambda bi, i: (bi, jnp.maximum(i * hb - 1, 0), 0)),
                  pl.BlockSpec((1, CF_HALO, CF_IN), lambda bi, i: (bi, jnp.minimum((i + 1) * hb, nh - 1), 0)),
                  pl.BlockSpec((CF_CONV, CF_W), const), vec, vec, vec],
        out_specs=pl.BlockSpec((1, tc, CF_W), lambda bi, i: (bi, i, 0)),
        out_shape=jax.ShapeDtypeStruct((b, l, CF_W), bf16),
        scratch_shapes=[pltpu.VMEM((SUBLANES, tc + 2 * CF_HALO, CF_W), f32)],
        compiler_params=_params("parallel", "parallel"),
        name="conformer_conv",
    )(u_cf, u_cf, u_cf, dw_w, dw_b.reshape(1, CF_W), ln_g.reshape(1, CF_W), ln_b.reshape(1, CF_W))


def _attn_body(nsub, tq, ck, n_lat, lam_init, lam_ref, g_ref, q_ref, kc_ref, vc_ref, *rest):
    if n_lat:
        kl_ref, vl_ref = rest[:2]
        rest = rest[2:]
    o_ref, s_a, s_b, mpart_ref, mrow_ref, lpart_ref, acc_ref = rest
    s_bufs = (s_a, s_b)
    n_ctx = kc_ref.shape[1]
    lane = lax.broadcasted_iota(i32, (1, DA_VD), 1)
    lp = lam_ref[...]
    lam = (jnp.exp(jnp.sum(lp[0:1] * lp[1:2], axis=-1, keepdims=True))
           - jnp.exp(jnp.sum(lp[2:3] * lp[3:4], axis=-1, keepdims=True)) + lam_init)

    def stage(t):
        do1, do2 = t < nsub, t >= 1
        s_w, s_r = s_bufs[t % 2], s_bufs[(t - 1) % 2]
        if do1:
            q = q_ref[0, t * tq:(t + 1) * tq, :]
            qm = [jnp.where(lane < DA_HD, q, jnp.zeros_like(q)), jnp.where(lane >= DA_HD, q, jnp.zeros_like(q))]
            mpart_ref[...] = jnp.full(mpart_ref.shape, -jnp.inf, f32)
        if do2:
            lpart_ref[...] = jnp.zeros(lpart_ref.shape, f32)
            acc_ref[...] = jnp.zeros(acc_ref.shape, f32)

        def chunk(k, vt, off, width):
            nb = width // DA_VD
            cols = [slice(off + j * DA_VD, off + (j + 1) * DA_VD) for j in range(nb)]
            if do1:
                for m in range(2):
                    s = _dot_nt(qm[m], k)
                    pm = s[:, 0:DA_VD]
                    for j in range(nb):
                        blk = s[:, j * DA_VD:(j + 1) * DA_VD]
                        s_w[m, :, cols[j]] = blk
                        if j:
                            pm = jnp.maximum(pm, blk)
                    mpart_ref[m] = jnp.maximum(mpart_ref[m], pm)
            if do2:
                for m in range(2):
                    mr = mrow_ref[m]
                    ls = lpart_ref[m]
                    parts = []
                    for j in range(nb):
                        e = jnp.exp2(s_r[m, :, cols[j]] - mr)
                        ls = ls + e
                        parts.append(e.astype(bf16))
                    lpart_ref[m] = ls
                    p = jnp.concatenate(parts, axis=1) if nb > 1 else parts[0]
                    acc_ref[m] += _dot_nt(vt, p)

        chunk(kc_ref[0], vc_ref[0], 0, n_ctx)
        for c in range(n_lat // ck if n_lat else 0):
            chunk(kl_ref[0, c * ck:(c + 1) * ck, :], vl_ref[0, :, c * ck:(c + 1) * ck], n_ctx + c * ck, ck)

        if do2:
            l0 = jnp.sum(lpart_ref[0], axis=-1, keepdims=True)
            l1 = jnp.sum(lpart_ref[1], axis=-1, keepdims=True)
            o = acc_ref[0].T * (1.0 / l0) - acc_ref[1].T * (lam / l1)
            ms = jnp.mean(o * o, axis=-1, keepdims=True)
            o_ref[0, (t - 1) * tq:t * tq, :] = (o * lax.rsqrt(ms + EPS) * g_ref[...] * (1.0 - lam_init)).astype(bf16)
        if do1:
            for m in range(2):
                mrow_ref[m] = jnp.broadcast_to(jnp.max(mpart_ref[m], axis=-1, keepdims=True), (tq, DA_VD))

    for t in range(nsub + 1):
        stage(t)


def _diff_attn(q, kv_ctx, kv_lat, lam_p, subln_g, lam_init):
    b, lq, _ = q.shape
    tq = min(ATT_TQ, lq)
    nsub = min(ATT_NSUB, lq // tq)
    rows = nsub * tq
    n_ctx = kv_ctx[0].shape[1]
    n_lat = kv_lat[0].shape[1] if kv_lat is not None else 0
    ck = min(ATT_CK, n_lat) if n_lat else 0
    lk = n_ctx + n_lat
    head = lambda bi, h, i: (bi, 0, h)
    head_t = lambda bi, h, i: (bi, h, 0)
    in_specs = [pl.BlockSpec((4, DA_HD), lambda bi, h, i: (0, 0)),
                pl.BlockSpec((1, DA_VD), lambda bi, h, i: (0, 0)),
                pl.BlockSpec((1, rows, DA_VD), lambda bi, h, i: (bi, i, h)),
                pl.BlockSpec((1, n_ctx, DA_VD), head), pl.BlockSpec((1, DA_VD, n_ctx), head_t)]
    args = [lam_p, subln_g.reshape(1, DA_VD), q, kv_ctx[0], jnp.swapaxes(kv_ctx[1], 1, 2)]
    if n_lat:
        in_specs += [pl.BlockSpec((1, n_lat, DA_VD), head), pl.BlockSpec((1, DA_VD, n_lat), head_t)]
        args += [kv_lat[0], jnp.swapaxes(kv_lat[1], 1, 2)]
    big = pltpu.VMEM((2, tq, lk), f32)
    pair = pltpu.VMEM((2, tq, DA_VD), f32)
    return pl.pallas_call(
        functools.partial(_attn_body, nsub, tq, ck, n_lat, lam_init),
        grid=(b, DA_HEADS, lq // rows),
        in_specs=in_specs,
        out_specs=pl.BlockSpec((1, rows, DA_VD), lambda bi, h, i: (bi, i, h)),
        out_shape=jax.ShapeDtypeStruct((b, lq, DA_W), bf16),
        scratch_shapes=[big, big, pair, pair, pair, pltpu.VMEM((2, DA_VD, tq), f32)],
        compiler_params=_params("parallel", "parallel", "parallel"),
        name="diff_attn_lat" if n_lat else "diff_attn_ctx",
    )(*args)


def _outproj_body(hy_ref, cf_ref, da_ref, w_ref, x_ref, g2_ref, gn_ref, sh_ref, sc_ref, wr_hi_ref, wr_lo_ref,
                  br_ref, tri_ref, xo_ref, rows_ref, bucket_ref, rank_ref, cnt_ref, carry_ref):
    first = (pl.program_id(0) == 0) & (pl.program_id(1) == 0)

    @pl.when(first)
    def _():
        carry_ref[...] = jnp.zeros(carry_ref.shape, f32)

    y = (_dot(hy_ref[0], w_ref[0:HY_W, :]) + _dot(cf_ref[0], w_ref[HY_W:HY_W + CF_W, :])
         + _dot(da_ref[0], w_ref[HY_W + CF_W:MIX_W, :]))
    x = x_ref[0] + g2_ref[0] * y
    xo_ref[0] = x
    ms = jnp.mean(x * x, axis=-1, keepdims=True)
    h = x * lax.rsqrt(ms + EPS) * gn_ref[...]
    h = h * (1.0 + sc_ref[0]) + sh_ref[0]
    rows_ref[0, :, 0:D_MODEL] = h
    tm = h.shape[0]

    h_hi, h_lo = _split_bf16(h)
    logits = _dot_nt(wr_hi_ref[...], h_hi) + _dot_nt(wr_hi_ref[...], h_lo) + _dot_nt(wr_lo_ref[...], h_hi)
    scores = jax.nn.sigmoid(logits)
    sel = scores + br_ref[...]
    srow = [sel[e:e + 1, :] for e in range(N_EXPERTS)]
    crow = [scores[e:e + 1, :] for e in range(N_EXPERTS)]

    best = None
    for g in range(N_GROUPS):
        a, b, c, d = srow[EXPERTS_PER_GROUP * g:EXPERTS_PER_GROUP * (g + 1)]
        hi1, lo1, hi2, lo2 = jnp.maximum(a, b), jnp.minimum(a, b), jnp.maximum(c, d), jnp.minimum(c, d)
        gs = jnp.maximum(hi1, hi2) + jnp.maximum(jnp.minimum(hi1, hi2), jnp.maximum(lo1, lo2))
        if best is None:
            best, gb = gs, jnp.zeros(gs.shape, i32)
        else:
            better = gs > best
            best = jnp.where(better, gs, best)
            gb = jnp.where(better, g, gb)

    def pick(rows, j):
        out = rows[j]
        for g in range(1, N_GROUPS):
            out = jnp.where(gb == g, rows[EXPERTS_PER_GROUP * g + j], out)
        return out

    v = [pick(srow, j) for j in range(EXPERTS_PER_GROUP)]
    sc = [pick(crow, j) for j in range(EXPERTS_PER_GROUP)]

    def argmax_first(vals):
        idx, m = jnp.zeros(vals[0].shape, i32), vals[0]
        for j in range(1, len(vals)):
            better = vals[j] > m
            idx = jnp.where(better, j, idx)
            m = jnp.where(better, vals[j], m)
        return idx

    i1 = argmax_first(v)
    i2 = argmax_first([jnp.where(i1 == j, -jnp.inf, v[j]) for j in range(EXPERTS_PER_GROUP)])
    s1 = functools.reduce(lambda p, q: p + q, [jnp.where(i1 == j, sc[j], 0.0) for j in range(EXPERTS_PER_GROUP)])
    s2 = functools.reduce(lambda p, q: p + q, [jnp.where(i2 == j, sc[j], 0.0) for j in range(EXPERTS_PER_GROUP)])
    inv = 1.0 / (s1 + s2)
    gate = [jnp.where(i1 == j, s1 * inv, 0.0) + jnp.where(i2 == j, s2 * inv, 0.0) for j in range(EXPERTS_PER_GROUP)]

    lo, hi = jnp.minimum(i1, i2), jnp.maximum(i1, i2)
    pair = jnp.where(lo == 0, hi - 1, jnp.where(lo == 1, hi + 1, N_PAIRS - 1))
    bucket = gb * N_PAIRS + pair
    g_lo = functools.reduce(lambda p, q: p + q, [jnp.where(lo == j, gate[j], 0.0) for j in range(EXPERTS_PER_GROUP)])
    g_hi = functools.reduce(lambda p, q: p + q, [jnp.where(hi == j, gate[j], 0.0) for j in range(EXPERTS_PER_GROUP)])

    memb = jnp.concatenate([(bucket == k).astype(f32) for k in range(N_BUCKETS)]
                           + [jnp.zeros((BUCKET_ROWS - N_BUCKETS, tm), f32)], axis=0)
    before = _dot(memb.astype(bf16), tri_ref[...])
    rank = jnp.sum(memb * (before + carry_ref[:, 0:1]), axis=0, keepdims=True)
    carry_ref[...] = carry_ref[...] + jnp.sum(memb, axis=1, keepdims=True)
    cnt_ref[...] = carry_ref[...]
    bucket_ref[0, 0] = bucket
    rank_ref[0, 0] = rank.astype(i32)

    tail = jnp.concatenate([g_lo, g_hi, jnp.zeros((MOE_TAIL - 2, tm), f32)], axis=0)
    rows_ref[0, :, D_MODEL:MOE_ROW] = tail.T


def _outproj_router(hy, cf, da, w_out_bf, x, gate2, norm2_g, shift2, scale2, wrt_hi, wrt_lo, b_router_col):
    b, l, d = x.shape
    tm = min(512, l)
    row = lambda bi, i: (bi, i, 0)
    per_b = lambda bi, i: (bi, 0, 0)
    const = lambda bi, i: (0, 0)
    mod = pl.BlockSpec((1, 1, d), per_b)
    tri = (jnp.arange(tm, dtype=i32)[:, None] < jnp.arange(tm, dtype=i32)[None, :]).astype(bf16)
    idx_spec = pl.BlockSpec((1, 1, 1, tm), lambda bi, i: (bi, i, 0, 0))
    idx_shape = jax.ShapeDtypeStruct((b, l // tm, 1, tm), i32)
    xo, rows, bucket, rank, cnt = pl.pallas_call(
        _outproj_body,
        grid=(b, l // tm),
        in_specs=[pl.BlockSpec((1, tm, HY_W), row), pl.BlockSpec((1, tm, CF_W), row), pl.BlockSpec((1, tm, DA_W), row),
                  pl.BlockSpec((MIX_W, d), const), pl.BlockSpec((1, tm, d), row), mod,
                  pl.BlockSpec((1, d), const), mod, mod,
                  pl.BlockSpec((N_EXPERTS, d), const), pl.BlockSpec((N_EXPERTS, d), const),
                  pl.BlockSpec((N_EXPERTS, 1), const), pl.BlockSpec((tm, tm), const)],
        out_specs=[pl.BlockSpec((1, tm, d), row), pl.BlockSpec((1, tm, MOE_ROW), row), idx_spec, idx_spec,
                   pl.BlockSpec((BUCKET_ROWS, DA_VD), const)],
        out_shape=[jax.ShapeDtypeStruct((b, l, d), f32), jax.ShapeDtypeStruct((b, l, MOE_ROW), f32), idx_shape, idx_shape,
                   jax.ShapeDtypeStruct((BUCKET_ROWS, DA_VD), f32)],
        scratch_shapes=[pltpu.VMEM((BUCKET_ROWS, DA_VD), f32)],
        compiler_params=_params("arbitrary", "arbitrary"),
        name="outproj_router",
    )(hy, cf, da, w_out_bf, x, gate2, norm2_g, shift2, scale2, wrt_hi, wrt_lo, b_router_col, tri)
    nt = b * l // tm
    return xo, rows.reshape(b * l, MOE_ROW), bucket.reshape(nt, 1, tm), rank.reshape(nt, 1, tm), cnt


def _row_copy(src_ref, src_row, dst_ref, dst_row, sem):
    return pltpu.make_async_copy(src_ref.at[pl.ds(src_row, 1)], dst_ref.at[pl.ds(dst_row, 1)], sem)


def _dispatch_body(slot_ref, rows_ref, xs_ref, sem):
    tm = rows_ref.shape[0]

    def issue(r, carry):
        _row_copy(rows_ref, r, xs_ref, slot_ref[0, 0, r], sem).start()
        return carry

    lax.fori_loop(0, tm, issue, 0, unroll=8)
    pltpu.make_async_copy(rows_ref, xs_ref.at[pl.ds(0, tm)], sem).wait()


def _dispatch(rows, slot, n_rows):
    n_tok = rows.shape[0]
    tm = slot.shape[2]
    return pl.pallas_call(
        _dispatch_body,
        grid=(n_tok // tm,),
        in_specs=[pl.BlockSpec((1, 1, tm), lambda i: (i, 0, 0), memory_space=pltpu.SMEM),
                  pl.BlockSpec((tm, MOE_ROW), lambda i: (i, 0))],
        out_specs=pl.BlockSpec(memory_space=pl.ANY),
        out_shape=jax.ShapeDtypeStruct((n_rows, MOE_ROW), f32),
        scratch_shapes=[pltpu.SemaphoreType.DMA],
        compiler_params=_params("arbitrary"),
        name="moe_dispatch",
    )(slot, rows)


def _ffn_body(oblk_ref, e0_ref, e1_ref, nrow_ref, xs_ref, w1_ref, w3_ref, w2_ref, ys_ref, acc_ref):
    i = pl.program_id(0)
    j = pl.program_id(1)
    nrow = nrow_ref[i]

    @pl.when(nrow > 0)
    def _():
        tmf = xs_ref.shape[0]
        valid = lax.broadcasted_iota(i32, (tmf, 1), 0) < nrow
        h = jnp.where(valid, xs_ref[:, 0:D_MODEL], 0.0).astype(bf16)
        tail = xs_ref[:, D_MODEL:MOE_ROW]
        lane = lax.broadcasted_iota(i32, tail.shape, 1)
        gate = jnp.sum(jnp.where(valid & (lane == j), tail, 0.0), axis=-1, keepdims=True)
        a = _dot(h, w1_ref[0])
        a = a * jax.nn.sigmoid(a) * _dot(h, w3_ref[0])
        y = gate * _dot(a.astype(bf16), w2_ref[0])

        @pl.when(j == 0)
        def _():
            acc_ref[...] = y

        @pl.when(j > 0)
        def _():
            acc_ref[...] += y

        @pl.when(j == TOP_K - 1)
        def _():
            ys_ref[...] = acc_ref[...]


def _ffn(xs, oblk, e0, e1, nrow, w1_bf, w3_bf, w2_bf, tmf):
    n_rows = xs.shape[0]
    d = D_MODEL
    wmap = lambda i, j, oblk, e0, e1, nrow: (jnp.where(j == 0, e0[i], e1[i]), 0, 0)
    grid_spec = pltpu.PrefetchScalarGridSpec(
        num_scalar_prefetch=4,
        grid=(n_rows // tmf, TOP_K),
        in_specs=[pl.BlockSpec((tmf, MOE_ROW), lambda i, j, oblk, e0, e1, nrow: (i, 0)),
                  pl.BlockSpec((1, d, D_EXPERT), wmap), pl.BlockSpec((1, d, D_EXPERT), wmap),
                  pl.BlockSpec((1, D_EXPERT, d), wmap)],
        out_specs=pl.BlockSpec((tmf, d), lambda i, j, oblk, e0, e1, nrow: (oblk[i], 0)),
        scratch_shapes=[pltpu.VMEM((tmf, d), f32)])
    return pl.pallas_call(
        _ffn_body,
        grid_spec=grid_spec,
        out_shape=jax.ShapeDtypeStruct((n_rows + tmf, d), f32),
        compiler_params=_params("arbitrary", "arbitrary"),
        name="moe_ffn",
    )(oblk, e0, e1, nrow, xs, w1_bf, w3_bf, w2_bf)


def _combine_body(slot_ref, ys_ref, x_ref, g5_ref, o_ref, buf_ref, sem):
    tm = buf_ref.shape[0]

    def issue(r, carry):
        _row_copy(ys_ref, slot_ref[0, 0, r], buf_ref, r, sem).start()
        return carry

    lax.fori_loop(0, tm, issue, 0, unroll=8)
    pltpu.make_async_copy(ys_ref.at[pl.ds(0, tm)], buf_ref, sem).wait()
    o_ref[0] = x_ref[0] + g5_ref[0] * buf_ref[...]


def _combine(ys, slot, x, gate5):
    b, l, d = x.shape
    tm = slot.shape[2]
    nt = l // tm
    return pl.pallas_call(
        _combine_body,
        grid=(b, nt),
        in_specs=[pl.BlockSpec((1, 1, tm), lambda bi, i: (bi * nt + i, 0, 0), memory_space=pltpu.SMEM),
                  pl.BlockSpec(memory_space=pl.ANY),
                  pl.BlockSpec((1, tm, d), lambda bi, i: (bi, i, 0)),
                  pl.BlockSpec((1, 1, d), lambda bi, i: (bi, 0, 0))],
        out_specs=pl.BlockSpec((1, tm, d), lambda bi, i: (bi, i, 0)),
        out_shape=jax.ShapeDtypeStruct((b, l, d), f32),
        scratch_shapes=[pltpu.VMEM((tm, d), f32), pltpu.SemaphoreType.DMA],
        compiler_params=_params("arbitrary", "arbitrary"),
        name="moe_combine",
    )(slot, ys, x, gate5)


def _tile_tables(cnt, tmf, n_tiles):
    counts = cnt[:N_BUCKETS, 0].astype(i32)
    tiles = (counts + tmf - 1) // tmf
    ends = jnp.cumsum(tiles)
    starts = ends - tiles
    i = jnp.arange(n_tiles, dtype=i32)
    bkt = jnp.minimum(jnp.sum((i[:, None] >= ends[None, :]).astype(i32), axis=1), N_BUCKETS - 1)
    k = i - starts[bkt]
    used = i < ends[-1]
    oblk = jnp.where(used, i, n_tiles)
    nrow = jnp.where(used, jnp.clip(counts[bkt] - k * tmf, 0, tmf), 0)
    first = jnp.asarray(PAIR_LO, i32)[bkt % N_PAIRS] + EXPERTS_PER_GROUP * (bkt // N_PAIRS)
    second = jnp.asarray(PAIR_HI, i32)[bkt % N_PAIRS] + EXPERTS_PER_GROUP * (bkt // N_PAIRS)
    return starts * tmf, oblk.astype(i32), first.astype(i32), second.astype(i32), nrow.astype(i32)


def _moe(rows, bucket, rank, cnt, x, gate5, w1_bf, w3_bf, w2_bf):
    n_tok = rows.shape[0]
    tmf = min(MOE_TMF, n_tok)
    n_tiles = n_tok // tmf + N_BUCKETS
    first_row, oblk, e0, e1, nrow = _tile_tables(cnt, tmf, n_tiles)
    slot = first_row[bucket] + rank
    xs = _dispatch(rows, slot, n_tiles * tmf)
    ys = _ffn(xs, oblk, e0, e1, nrow, w1_bf, w3_bf, w2_bf, tmf)
    return _combine(ys, slot, x, gate5)


def _rope_tables(n_lat):
    rows = n_lat // GRID_W
    row = jnp.repeat(jnp.arange(rows, dtype=f32), GRID_W)
    col = jnp.tile(jnp.arange(GRID_W, dtype=f32), rows)
    inv = ROPE_THETA ** (-jnp.arange(ROPE_AXIS_FREQS, dtype=f32) / ROPE_AXIS_FREQS)
    ang = jnp.concatenate([row[:, None] * inv, col[:, None] * inv], axis=-1)
    cos, sin = jnp.cos(ang), jnp.sin(ang)
    reps = DA_QK // DA_HD
    return jnp.tile(jnp.concatenate([cos, cos], -1), (1, reps)), jnp.tile(jnp.concatenate([-sin, sin], -1), (1, reps))


def _dft_tables(l):
    n = 2 * l
    f = jnp.arange(l, dtype=i32)[:, None]
    t = jnp.arange(l, dtype=i32)[None, :]
    ang = ((f * t) % n).astype(f32) * (2.0 * math.pi / n)
    cm = jnp.cos(ang)
    sm = jnp.where(f == 0, jnp.where(t % 2 == 0, 1.0, -1.0), jnp.sin(ang))
    return cm.astype(bf16), sm.astype(bf16), sm.T.astype(bf16)


def _hy_features(l):
    t_idx = jnp.arange(l, dtype=f32)[:, None]
    t01 = t_idx / max(l - 1, 1)
    bands = jnp.linspace(1e-4, HY_BANDS - 1, HY_BANDS, dtype=f32)
    ang = 2.0 * math.pi * bands * t_idx / l
    feats = jnp.concatenate([t01, jnp.cos(ang), -jnp.sin(ang)], axis=-1)
    return jnp.pad(feats, ((0, 0), (0, HY_EMB_PAD - HY_EMB)))


def _hyena(u_hy, dft, feats, deltas, conv_w, conv_b, w1p, b1, w2, b2, w3, freq, bias):
    cm, sm, smt = dft
    x0, z = _hyprep(u_hy, conv_w, conv_b)
    hs, hd = _hy_filter(feats, w1p, b1, w2, b2, w3, freq, deltas)
    a, bco, dco = _hy_spectrum(cm, sm, hs, hd)
    return _fftconv(z, x0, cm, sm, smt, a, bco, dco, bias)


def kernel(x, c, ctx, c_ctx, w_ada, b_ada, norm1_g, norm2_g, w_in, hy_conv_w, hy_conv_b, hy_w1, hy_b1, hy_w2, hy_b2,
           hy_w3, hy_freq, hy_bias, cf_dw_w, cf_dw_b, cf_ln_g, cf_ln_b, da_qn_g, da_kn_g, da_lam, da_subln_g, w_out,
           w_router, b_router, moe_w1, moe_w3, moe_w2):
    depth = w_ada.shape[0]
    bsz, n_lat, d = x.shape
    n_ctx = ctx.shape[1]

    rope = _rope_tables(n_lat)
    dft_lat, dft_ctx = _dft_tables(n_lat), _dft_tables(n_ctx)
    feats_lat, feats_ctx = _hy_features(n_lat), _hy_features(n_ctx)
    deltas = jnp.abs(jnp.linspace(HY_MIN_DECAY, HY_MAX_DECAY, HY_W, dtype=f32)).reshape(1, HY_W)
    seg = jnp.arange(DA_QK, dtype=i32) // DA_HD
    bd = (seg[:, None] == seg[None, :]).astype(bf16)

    rows = jnp.concatenate([c, c_ctx[None], jnp.zeros((7, d), f32)], axis=0)
    mod_all = _ada(rows, w_ada, b_ada)

    wr_hi = w_router.T.astype(bf16)
    wr_lo = (w_router.T - wr_hi.astype(f32)).astype(bf16)
    br = b_router.reshape(N_EXPERTS, 1)
    hy_w1p = jnp.pad(hy_w1, ((0, 0), (0, HY_EMB_PAD - HY_EMB), (0, 0)))

    x_lat, x_ctx = x, ctx
    for l in range(depth):
        last = l == depth - 1
        lam_init = 0.8 - 0.6 * math.exp(-0.3 * l)
        mod = mod_all[l, :bsz].reshape(bsz, 6, 1, d)
        mod_c = jnp.broadcast_to(mod_all[l, bsz].reshape(1, 6, 1, d), (bsz, 6, 1, d))
        w_in_bf = w_in[l].astype(bf16)
        w_out_bf = w_out[l].astype(bf16)
        w1_bf, w3_bf, w2_bf = moe_w1[l].astype(bf16), moe_w3[l].astype(bf16), moe_w2[l].astype(bf16)
        g1 = norm1_g[l].reshape(1, d)
        g2 = norm2_g[l].reshape(1, d)
        qg = jnp.tile(da_qn_g[l], DA_QK // DA_HD).reshape(1, DA_QK)
        kg = jnp.tile(da_kn_g[l], DA_QK // DA_HD).reshape(1, DA_QK)
        hy_p = (hy_conv_w[l], hy_conv_b[l], hy_w1p[l], hy_b1[l], hy_w2[l], hy_b2[l], hy_w3[l], hy_freq[l], hy_bias[l])
        cf_p = (cf_dw_w[l], cf_dw_b[l], cf_ln_g[l], cf_ln_b[l])

        uhy, ucf, q_lat, k_lat, v_lat = _inproj(x_lat, mod[:, 0], mod[:, 1], g1, w_in_bf, qg, kg, bd, rope)
        uhy_c, ucf_c, q_ctx, k_ctx, v_ctx = _inproj(x_ctx, mod_c[:, 0], mod_c[:, 1], g1, w_in_bf, qg, kg, bd, None)
        hy_lat = _hyena(uhy, dft_lat, feats_lat, deltas, *hy_p)
        cf_lat = _conformer(ucf, *cf_p)
        da_lat = _diff_attn(q_lat, (k_ctx, v_ctx), (k_lat, v_lat), da_lam[l], da_subln_g[l], lam_init)
        x_lat, *routed = _outproj_router(hy_lat, cf_lat, da_lat, w_out_bf, x_lat, mod[:, 2], g2, mod[:, 3], mod[:, 4],
                                         wr_hi, wr_lo, br)
        x_lat = _moe(*routed, x_lat, mod[:, 5], w1_bf, w3_bf, w2_bf)
        if not last:
            hy_c = _hyena(uhy_c, dft_ctx, feats_ctx, deltas, *hy_p)
            cf_c = _conformer(ucf_c, *cf_p)
            da_c = _diff_attn(q_ctx, (k_ctx, v_ctx), None, da_lam[l], da_subln_g[l], lam_init)
            x_ctx, *routed_c = _outproj_router(hy_c, cf_c, da_c, w_out_bf, x_ctx, mod_c[:, 2], g2, mod_c[:, 3],
                                               mod_c[:, 4], wr_hi, wr_lo, br)
            x_ctx = _moe(*routed_c, x_ctx, mod_c[:, 5], w1_bf, w3_bf, w2_bf)
    return x_lat
```
